```python
import math
import jax, jax.numpy as jnp
from jax import lax
import numpy as np

D_MODEL = 1024
BATCH = 16
SEQ = 4096
DEPTH = 1

DIFF_HEADS = 4
DIFF_HEAD_DIM = 64
DIFF_WIDTH = DIFF_HEADS * 2 * DIFF_HEAD_DIM
FOX_HEADS = 8
FOX_HEAD_DIM = 64
FOX_WIDTH = FOX_HEADS * FOX_HEAD_DIM
MIX_WIDTH = DIFF_WIDTH + FOX_WIDTH
IN_COLS = 3 * DIFF_WIDTH + 3 * FOX_WIDTH + FOX_HEADS
D_FF = 4 * D_MODEL
N_BUCKETS = 32
MAX_DISTANCE = 128
Q_BLOCK = 128
EPS = 1e-6
NEG = -1e30

kernel_name = "hybrid_diffattn_fox_sqrelu_sandwich"


def rmsnorm(x, g):
    xf = x.astype(jnp.float32)
    y = xf * lax.rsqrt(jnp.mean(xf * xf, axis=-1, keepdims=True) + EPS)
    return (y * g.astype(jnp.float32)).astype(x.dtype)


def t5_bucket(dist):
    max_exact = N_BUCKETS // 2
    d = jnp.maximum(dist, 1).astype(jnp.float32)
    large = max_exact + (jnp.log(d / max_exact) / math.log(MAX_DISTANCE / max_exact)
                         * (N_BUCKETS - max_exact))
    large = jnp.minimum(large.astype(jnp.int32), N_BUCKETS - 1)
    return jnp.where(dist < max_exact, dist, large)


def token_mixer(h, w_in, b_f, lam_q1, lam_k1, lam_q2, lam_k2, subln_g, rel_bias, lambda_init):
    B, S, _ = h.shape
    proj = h @ w_in
    o = 0
    dq = proj[..., o:o + DIFF_WIDTH].reshape(B, S, 2 * DIFF_HEADS, DIFF_HEAD_DIM); o += DIFF_WIDTH
    dk = proj[..., o:o + DIFF_WIDTH].reshape(B, S, 2 * DIFF_HEADS, DIFF_HEAD_DIM); o += DIFF_WIDTH
    dv = proj[..., o:o + DIFF_WIDTH].reshape(B, S, DIFF_HEADS, 2 * DIFF_HEAD_DIM); o += DIFF_WIDTH
    fq = proj[..., o:o + FOX_WIDTH].reshape(B, S, FOX_HEADS, FOX_HEAD_DIM); o += FOX_WIDTH
    fk = proj[..., o:o + FOX_WIDTH].reshape(B, S, FOX_HEADS, FOX_HEAD_DIM); o += FOX_WIDTH
    fv = proj[..., o:o + FOX_WIDTH].reshape(B, S, FOX_HEADS, FOX_HEAD_DIM); o += FOX_WIDTH
    f_logit = proj[..., o:o + FOX_HEADS].astype(jnp.float32) + b_f.astype(jnp.float32)
    cum = jnp.cumsum(jax.nn.log_sigmoid(f_logit), axis=1)
    cum_k = jnp.transpose(cum, (0, 2, 1))

    lam = (jnp.exp(jnp.sum(lam_q1.astype(jnp.float32) * lam_k1.astype(jnp.float32)))
           - jnp.exp(jnp.sum(lam_q2.astype(jnp.float32) * lam_k2.astype(jnp.float32)))
           + lambda_init)

    dist_bias = rel_bias[t5_bucket(jnp.arange(S))]
    kpos = jnp.arange(S)
    nb = S // Q_BLOCK

    def to_blocks(a):
        return jnp.swapaxes(a.reshape(B, nb, Q_BLOCK, *a.shape[2:]), 0, 1)

    def block(args):
        i, qd, qf, cq = args
        qpos = i * Q_BLOCK + jnp.arange(Q_BLOCK)
        rel = qpos[:, None] - kpos[None, :]
        causal = rel >= 0
        sd = jnp.einsum('bqhd,bkhd->bhqk', qd, dk).astype(jnp.float32) * (DIFF_HEAD_DIM ** -0.5)
        bias = jnp.transpose(dist_bias[jnp.maximum(rel, 0)], (2, 0, 1)).astype(jnp.float32)
        sd = sd.reshape(B, DIFF_HEADS, 2, Q_BLOCK, S) + bias[:, None]
        pd = jax.nn.softmax(jnp.where(causal, sd, NEG), axis=-1)
        ad = pd[:, :, 0] - lam * pd[:, :, 1]
        od = jnp.einsum('bhqk,bkhe->bqhe', ad.astype(dv.dtype), dv)
        od = rmsnorm(od, subln_g) * (1.0 - lambda_init)
        sf = jnp.einsum('bqhd,bkhd->bhqk', qf, fk).astype(jnp.float32) * (FOX_HEAD_DIM ** -0.5)
        sf = sf + jnp.transpose(cq, (0, 2, 1))[..., :, None] - cum_k[..., None, :]
        pf = jax.nn.softmax(jnp.where(causal, sf, NEG), axis=-1)
        of = jnp.einsum('bhqk,bkhd->bqhd', pf.astype(fv.dtype), fv)
        return jnp.concatenate([od.reshape(B, Q_BLOCK, DIFF_WIDTH),
                                of.reshape(B, Q_BLOCK, FOX_WIDTH)], axis=-1)

    out = lax.map(block, (jnp.arange(nb), to_blocks(dq), to_blocks(fq), to_blocks(cum)))
    return jnp.swapaxes(out, 0, 1).reshape(B, S, MIX_WIDTH)


def setup_inputs(seed: int = 0) -> dict:
    key = jax.random.key(seed)
    ks = jax.random.split(key, 20)
    f32 = jnp.float32
    nrm = lambda k, shape, s: jax.random.normal(k, shape, f32) * s
    return {
        "x": jax.random.normal(ks[0], (BATCH, SEQ, D_MODEL), f32),
        "ln_attn_pre": 1.0 + nrm(ks[1], (DEPTH, D_MODEL), 0.05),
        "w_in": nrm(ks[2], (DEPTH, D_MODEL, IN_COLS), D_MODEL ** -0.5),
        "b_f": 2.0 + nrm(ks[3], (DEPTH, FOX_HEADS), 0.1),
        "lam_q1": nrm(ks[4], (DEPTH, DIFF_HEAD_DIM), 0.1),
        "lam_k1": nrm(ks[5], (DEPTH, DIFF_HEAD_DIM), 0.1),
        "lam_q2": nrm(ks[6], (DEPTH, DIFF_HEAD_DIM), 0.1),
        "lam_k2": nrm(ks[7], (DEPTH, DIFF_HEAD_DIM), 0.1),
        "subln_g": 1.0 + nrm(ks[8], (DEPTH, 2 * DIFF_HEAD_DIM), 0.05),
        "rel_bias": nrm(ks[9], (N_BUCKETS, DIFF_HEADS), 0.5),
        "w_out": nrm(ks[10], (DEPTH, MIX_WIDTH, D_MODEL), MIX_WIDTH ** -0.5),
        "ln_attn_post": 1.0 + nrm(ks[11], (DEPTH, D_MODEL), 0.05),
        "ln_mlp_pre": 1.0 + nrm(ks[12], (DEPTH, D_MODEL), 0.05),
        "w_up": nrm(ks[13], (DEPTH, D_MODEL, D_FF), D_MODEL ** -0.5),
        "w_down": nrm(ks[14], (DEPTH, D_FF, D_MODEL), D_FF ** -0.5),
        "ln_mlp_post": 1.0 + nrm(ks[15], (DEPTH, D_MODEL), 0.05),
    }


def reference(x, ln_attn_pre, w_in, b_f, lam_q1, lam_k1, lam_q2, lam_k2, subln_g, rel_bias,
              w_out, ln_attn_post, ln_mlp_pre, w_up, w_down, ln_mlp_post):
    for l in range(DEPTH):
        lambda_init = 0.8 - 0.6 * math.exp(-0.3 * l)
        h = rmsnorm(x, ln_attn_pre[l])
        mix = token_mixer(h, w_in[l], b_f[l], lam_q1[l], lam_k1[l], lam_q2[l], lam_k2[l],
                          subln_g[l], rel_bias, lambda_init)
        x = x + rmsnorm(mix @ w_out[l], ln_attn_post[l])
        h = rmsnorm(x, ln_mlp_pre[l])
        u = jnp.square(jax.nn.relu(h @ w_up[l]))
        x = x + rmsnorm(u @ w_down[l], ln_mlp_post[l])
    return x
```

```python
import functools
import math

import numpy as np
import jax
import jax.numpy as jnp
from jax import lax
from jax.experimental import pallas as pl
from jax.experimental.pallas import tpu as pltpu

D_MODEL = 1024
DIFF_HEADS = 4
DIFF_HEAD_DIM = 64
DIFF_WIDTH = DIFF_HEADS * 2 * DIFF_HEAD_DIM
FOX_HEADS = 8
FOX_HEAD_DIM = 64
FOX_WIDTH = FOX_HEADS * FOX_HEAD_DIM
FOX_UNITS = FOX_HEADS // 2
D_FF = 4 * D_MODEL
N_BUCKETS = 32
MAX_DISTANCE = 128
EPS = 1e-6
NEG = -1e30
QK_SCALE = DIFF_HEAD_DIM ** -0.5

TQ = 256
TK = 256
TM_PROJ = 512
TM_POST = 512
FF_CHUNK = 1024
XROWS = 16
VMEM_LIMIT = 56 * 1024 * 1024


def _rms(x, g):
    ms = jnp.mean(x * x, axis=-1, keepdims=True)
    return x * lax.rsqrt(ms + EPS) * g


def _proj_kernel(x_ref, g_ref, wrow_ref, wcol_ref, wf_ref, bf_ref, pq_ref, pk_ref, oq_ref, ok_ref,
                 kd_ref, kf_ref, qd_ref, vd_ref, qf_ref, vf_ref, xq_ref, carry_ref):
    tm = x_ref.shape[1]
    nsub = tm // TQ
    h = _rms(x_ref[0], g_ref[...]).astype(jnp.bfloat16)

    krow = jnp.dot(h, wrow_ref[...], preferred_element_type=jnp.float32)
    kd_ref[0] = krow[:, :DIFF_WIDTH].astype(jnp.bfloat16)

    nt = (((1,), (1,)), ((), ()))
    outs = (qd_ref, vd_ref, qf_ref, vf_ref)
    scales = (QK_SCALE, None, QK_SCALE, None)
    for n, (o_ref, sc) in enumerate(zip(outs, scales)):
        colt = lax.dot_general(wcol_ref[n * 512:(n + 1) * 512, :], h, nt,
                               preferred_element_type=jnp.float32)
        if sc is not None:
            colt = colt * sc
        colt = colt.astype(jnp.bfloat16)
        for c in range(nsub):
            o_ref[0, c] = colt[:, c * TQ:(c + 1) * TQ]

    ft = lax.dot_general(wf_ref[...], h, nt, preferred_element_type=jnp.float32) + bf_ref[...]
    ls = jnp.minimum(ft, 0.0) - jnp.log1p(jnp.exp(-jnp.abs(ft)))
    lane = lax.broadcasted_iota(jnp.int32, ls.shape, 1)
    sh = 1
    while sh < tm:
        ls = ls + jnp.where(lane >= sh, pltpu.roll(ls, sh, axis=1), 0.0)
        sh *= 2

    @pl.when(pl.program_id(1) == 0)
    def _():
        carry_ref[...] = jnp.zeros_like(carry_ref)

    cum = ls + carry_ref[...]
    carry_ref[...] = cum[:, tm - 1:tm]

    hi = cum.astype(jnp.bfloat16)
    r1 = cum - hi.astype(jnp.float32)
    mid = r1.astype(jnp.bfloat16)
    lo = (r1 - mid.astype(jnp.float32)).astype(jnp.bfloat16)
    parts = jnp.concatenate([hi, mid, lo, jnp.zeros((128 - 3 * XROWS, tm), jnp.bfloat16)], axis=0)
    xq = jnp.dot(pq_ref[...], parts, preferred_element_type=jnp.float32) + oq_ref[...]
    xq = xq.astype(jnp.bfloat16)
    for c in range(nsub):
        xq_ref[0, c] = xq[:, c * TQ:(c + 1) * TQ]
    xk = jnp.dot(pk_ref[...], parts, preferred_element_type=jnp.float32) + ok_ref[...]
    for u in range(FOX_UNITS):
        kf_ref[0, :, 256 * u:256 * u + 128] = (
            krow[:, DIFF_WIDTH + 128 * u:DIFF_WIDTH + 128 * (u + 1)].astype(jnp.bfloat16))
        kf_ref[0, :, 256 * u + 128:256 * (u + 1)] = (
            xk[128 * u:128 * (u + 1), :].T.astype(jnp.bfloat16))


def _placement_constants():
    pq = np.zeros((FOX_UNITS * 2 * XROWS, 128), np.float32)
    oq = np.zeros((FOX_UNITS * 2 * XROWS, 1), np.float32)
    pk = np.zeros((FOX_UNITS * 128, 128), np.float32)
    ok = np.zeros((FOX_UNITS * 128, 1), np.float32)
    for u in range(FOX_UNITS):
        for a in range(2):
            head = 2 * u + a
            for r in range(3):
                pq[2 * XROWS * u + XROWS * a + r, XROWS * r + head] = 1.0
                oq[2 * XROWS * u + XROWS * a + 3 + r, 0] = 1.0
                pk[128 * u + XROWS * a + 3 + r, XROWS * r + head] = -1.0
                ok[128 * u + XROWS * a + r, 0] = 1.0
    return (jnp.asarray(pq, jnp.bfloat16), jnp.asarray(pk, jnp.bfloat16),
            jnp.asarray(oq), jnp.asarray(ok))


def _proj(x, g, wrow, wcol, wf, bf_col):
    B, S, D = x.shape
    tm = TM_PROJ
    nsub = tm // TQ
    nq = S // TQ
    pq, pk, oq, ok = _placement_constants()
    const = lambda shape: pl.BlockSpec(shape, lambda b, t: (0,) * len(shape))
    colspec = lambda rows: pl.BlockSpec((1, nsub, rows, TQ), lambda b, t: (b, t, 0, 0))
    bf16 = jnp.bfloat16
    return pl.pallas_call(
        _proj_kernel,
        grid=(B, S // tm),
        in_specs=[
            pl.BlockSpec((1, tm, D), lambda b, t: (b, t, 0)),
            const((1, D)), const(wrow.shape), const(wcol.shape), const(wf.shape), const(bf_col.shape),
            const(pq.shape), const(pk.shape), const(oq.shape), const(ok.shape),
        ],
        out_specs=[
            pl.BlockSpec((1, tm, DIFF_WIDTH), lambda b, t: (b, t, 0)),
            pl.BlockSpec((1, tm, 2 * FOX_WIDTH), lambda b, t: (b, t, 0)),
            colspec(512), colspec(512), colspec(512), colspec(512), colspec(FOX_UNITS * 2 * XROWS),
        ],
        out_shape=[
            jax.ShapeDtypeStruct((B, S, DIFF_WIDTH), bf16),
            jax.ShapeDtypeStruct((B, S, 2 * FOX_WIDTH), bf16),
            jax.ShapeDtypeStruct((B, nq, 512, TQ), bf16),
            jax.ShapeDtypeStruct((B, nq, 512, TQ), bf16),
            jax.ShapeDtypeStruct((B, nq, 512, TQ), bf16),
            jax.ShapeDtypeStruct((B, nq, 512, TQ), bf16),
            jax.ShapeDtypeStruct((B, nq, FOX_UNITS * 2 * XROWS, TQ), bf16),
        ],
        scratch_shapes=[pltpu.VMEM((XROWS, 1), jnp.float32)],
        compiler_params=pltpu.CompilerParams(
            dimension_semantics=("arbitrary", "arbitrary"), vmem_limit_bytes=VMEM_LIMIT),
        name="proj",
    )(x, g, wrow, wcol, wf, bf_col, pq, pk, oq, ok)


def _t5_buckets(n):
    max_exact = N_BUCKETS // 2
    dist = np.arange(n)
    d = np.maximum(dist, 1).astype(np.float32)
    large = np.float32(max_exact) + (np.log(d / np.float32(max_exact))
                                     / np.float32(math.log(MAX_DISTANCE / max_exact))
                                     * np.float32(N_BUCKETS - max_exact))
    large = np.minimum(large.astype(np.int32), N_BUCKETS - 1)
    return np.where(dist < max_exact, dist, large).astype(np.int32)


def _bias_kernel(rel_ref, idx_ref, out_ref):
    hd = pl.program_id(0)
    for t in range(2):
        idx = idx_ref[t]
        val = jnp.full(idx.shape, NEG, jnp.float32)
        for bkt in range(N_BUCKETS):
            val = jnp.where(idx == bkt, rel_ref[bkt, hd], val)
        out_ref[0, t] = val


def _bias_tiles(rel_bias, S):
    buckets = _t5_buckets(S)
    far = int(buckets[-1])
    assert np.all(buckets[TQ + 1:] == far)
    s_loc = np.arange(TK)[:, None]
    t_loc = np.arange(TQ)[None, :]
    rel0 = t_loc - s_loc
    idx0 = np.where(rel0 >= 0, buckets[np.maximum(rel0, 0)], -1)
    idx1 = buckets[TQ + t_loc - s_loc]
    idx = jnp.asarray(np.stack([idx0, idx1]).astype(np.int32))
    tiles = pl.pallas_call(
        _bias_kernel,
        grid=(DIFF_HEADS,),
        in_specs=[pl.BlockSpec(memory_space=pltpu.SMEM),
                  pl.BlockSpec((2, TK, TQ), lambda h: (0, 0, 0))],
        out_specs=pl.BlockSpec((1, 2, TK, TQ), lambda h: (h, 0, 0, 0)),
        out_shape=jax.ShapeDtypeStruct((DIFF_HEADS, 2, TK, TQ), jnp.float32),
        name="t5_bias_tiles",
    )(rel_bias, idx)
    return tiles, far


def _flash_update(m_ref, l_ref, acc_ref, a, st, vt):
    m_old = m_ref[a]
    m_new = jnp.maximum(m_old, jnp.max(st, axis=0, keepdims=True))
    alpha = jnp.exp(m_old - m_new)
    p = jnp.exp(st - m_new)
    l_ref[a] = alpha * l_ref[a] + jnp.sum(p, axis=0, keepdims=True)
    acc_ref[a] = alpha * acc_ref[a] + jnp.dot(vt, p.astype(jnp.bfloat16),
                                              preferred_element_type=jnp.float32)
    m_ref[a] = m_new


def _flash_init(m_ref, l_ref, acc_ref):
    m_ref[...] = jnp.full(m_ref.shape, NEG, jnp.float32)
    l_ref[...] = jnp.zeros_like(l_ref)
    acc_ref[...] = jnp.zeros_like(acc_ref)


def _diff_kernel(far_bucket, lambda_init,
                 rel_ref, k_ref, q_ref, v_ref, bias_ref, lq1_ref, lk1_ref, lq2_ref, lk2_ref, g_ref,
                 o_ref, qpad_ref, m_ref, l_ref, acc_ref):
    hd = pl.program_id(1)
    i = pl.program_id(2)
    d = DIFF_HEAD_DIM
    zeros = jnp.zeros((d, TQ), jnp.bfloat16)
    qpad_ref[0, :d] = q_ref[0, 0, :d]
    qpad_ref[0, d:] = zeros
    qpad_ref[1, :d] = zeros
    qpad_ref[1, d:] = q_ref[0, 0, d:]
    _flash_init(m_ref, l_ref, acc_ref)
    far_bias = rel_ref[far_bucket, hd]

    def step(j, add_bias):
        kblk = k_ref[0, pl.ds(pl.multiple_of(j * TK, TK), TK), :]
        vt = v_ref[0, j]
        for a in range(2):
            st = jnp.dot(kblk, qpad_ref[a], preferred_element_type=jnp.float32)
            _flash_update(m_ref, l_ref, acc_ref, a, add_bias(st), vt)

    def far_body(j, carry):
        step(j, lambda st: st + far_bias)
        return carry

    lax.fori_loop(0, jnp.maximum(i - 1, 0), far_body, 0)

    @pl.when(i >= 1)
    def _():
        step(i - 1, lambda st: st + bias_ref[0, 1])

    step(i, lambda st: st + bias_ref[0, 0])

    lam = (jnp.exp(jnp.sum(lq1_ref[...] * lk1_ref[...], axis=-1, keepdims=True))
           - jnp.exp(jnp.sum(lq2_ref[...] * lk2_ref[...], axis=-1, keepdims=True))
           + lambda_init)
    o0 = acc_ref[0] * (1.0 / l_ref[0])
    o1 = acc_ref[1] * (1.0 / l_ref[1])
    od = o0 - lam * o1
    ms = jnp.mean(od * od, axis=0, keepdims=True)
    y = (od * lax.rsqrt(ms + EPS)).T * g_ref[...]
    o_ref[0] = (y * (1.0 - lambda_init)).astype(o_ref.dtype)


def _diff_attention(kd, qd, vd, bias_tiles, far_bucket, rel_bias, lq1, lk1, lq2, lk2, g, lambda_init):
    B, S, _ = kd.shape
    nq = S // TQ
    vec = lambda n: pl.BlockSpec((1, n), lambda b, h, i: (0, 0))
    return pl.pallas_call(
        functools.partial(_diff_kernel, far_bucket, lambda_init),
        grid=(B, DIFF_HEADS, nq),
        in_specs=[
            pl.BlockSpec(memory_space=pltpu.SMEM),
            pl.BlockSpec((1, S, 128), lambda b, h, i: (b, 0, h)),
            pl.BlockSpec((1, 1, 128, TQ), lambda b, h, i: (b, i, h, 0)),
            pl.BlockSpec((1, nq, 128, TK), lambda b, h, i: (b, 0, h, 0)),
            pl.BlockSpec((1, 2, TK, TQ), lambda b, h, i: (h, 0, 0, 0)),
            vec(DIFF_HEAD_DIM), vec(DIFF_HEAD_DIM), vec(DIFF_HEAD_DIM), vec(DIFF_HEAD_DIM),
            vec(2 * DIFF_HEAD_DIM),
        ],
        out_specs=pl.BlockSpec((1, TQ, 128), lambda b, h, i: (b, i, h)),
        out_shape=jax.ShapeDtypeStruct((B, S, DIFF_WIDTH), jnp.bfloat16),
        scratch_shapes=[
            pltpu.VMEM((2, 128, TQ), jnp.bfloat16),
            pltpu.VMEM((2, 1, TQ), jnp.float32),
            pltpu.VMEM((2, 1, TQ), jnp.float32),
            pltpu.VMEM((2, 128, TQ), jnp.float32),
        ],
        compiler_params=pltpu.CompilerParams(
            dimension_semantics=("parallel", "parallel", "arbitrary"), vmem_limit_bytes=VMEM_LIMIT),
        name="diff_attention",
    )(rel_bias, kd, qd, vd, bias_tiles, lq1, lk1, lq2, lk2, g)


def _fox_kernel(k_ref, q_ref, xq_ref, v_ref, o_ref, qpad_ref, m_ref, l_ref, acc_ref):
    i = pl.program_id(2)
    d = FOX_HEAD_DIM
    qpad_ref[...] = jnp.zeros_like(qpad_ref)
    for a in range(2):
        qpad_ref[a, d * a:d * (a + 1)] = q_ref[0, 0, d * a:d * (a + 1)]
        qpad_ref[a, 128 + XROWS * a:128 + XROWS * (a + 1)] = xq_ref[0, 0, XROWS * a:XROWS * (a + 1)]
    _flash_init(m_ref, l_ref, acc_ref)

    def step(j, masked):
        kblk = k_ref[0, pl.ds(pl.multiple_of(j * TK, TK), TK), :]
        for a in range(2):
            st = jnp.dot(kblk, qpad_ref[a], preferred_element_type=jnp.float32)
            if masked:
                row = lax.broadcasted_iota(jnp.int32, st.shape, 0)
                col = lax.broadcasted_iota(jnp.int32, st.shape, 1)
                st = jnp.where(row > col, NEG, st)
            _flash_update(m_ref, l_ref, acc_ref, a, st, v_ref[0, j, d * a:d * (a + 1)])

    def body(j, carry):
        step(j, False)
        return carry

    lax.fori_loop(0, i, body, 0)
    step(i, True)

    o = jnp.concatenate([acc_ref[0] * (1.0 / l_ref[0]), acc_ref[1] * (1.0 / l_ref[1])], axis=0)
    o_ref[0] = o.T.astype(o_ref.dtype)


def _fox_attention(kf, qf, xq, vf):
    B, S, _ = kf.shape
    nq = S // TQ
    return pl.pallas_call(
        _fox_kernel,
        grid=(B, FOX_UNITS, nq),
        in_specs=[
            pl.BlockSpec((1, S, 256), lambda b, u, i: (b, 0, u)),
            pl.BlockSpec((1, 1, 128, TQ), lambda b, u, i: (b, i, u, 0)),
            pl.BlockSpec((1, 1, 2 * XROWS, TQ), lambda b, u, i: (b, i, u, 0)),
            pl.BlockSpec((1, nq, 128, TK), lambda b, u, i: (b, 0, u, 0)),
        ],
        out_specs=pl.BlockSpec((1, TQ, 128), lambda b, u, i: (b, i, u)),
        out_shape=jax.ShapeDtypeStruct((B, S, FOX_WIDTH), jnp.bfloat16),
        scratch_shapes=[
            pltpu.VMEM((2, 256, TQ), jnp.bfloat16),
            pltpu.VMEM((2, 1, TQ), jnp.float32),
            pltpu.VMEM((2, 1, TQ), jnp.float32),
            pltpu.VMEM((2, FOX_HEAD_DIM, TQ), jnp.float32),
        ],
        compiler_params=pltpu.CompilerParams(
            dimension_semantics=("parallel", "parallel", "arbitrary"), vmem_limit_bytes=VMEM_LIMIT),
        name="fox_attention",
    )(kf, qf, xq, vf)


def _post_kernel(md_ref, mf_ref, x_ref, wod_ref, wof_ref, wup_ref, wdn_ref, g1_ref, g2_ref, g3_ref,
                 o_ref):
    a = (jnp.dot(md_ref[...], wod_ref[...], preferred_element_type=jnp.float32)
         + jnp.dot(mf_ref[...], wof_ref[...], preferred_element_type=jnp.float32))
    x1 = x_ref[...] + _rms(a, g1_ref[...])
    h2 = _rms(x1, g2_ref[...]).astype(jnp.bfloat16)
    acc = None
    for c in range(D_FF // FF_CHUNK):
        u = jnp.dot(h2, wup_ref[:, c * FF_CHUNK:(c + 1) * FF_CHUNK], preferred_element_type=jnp.float32)
        u = jnp.square(jnp.maximum(u, 0.0)).astype(jnp.bfloat16)
        part = jnp.dot(u, wdn_ref[c * FF_CHUNK:(c + 1) * FF_CHUNK, :], preferred_element_type=jnp.float32)
        acc = part if acc is None else acc + part
    o_ref[...] = x1 + _rms(acc, g3_ref[...])


def _post(mix_d, mix_f, x2d, wod, wof, wup, wdn, g1, g2, g3):
    R, D = x2d.shape
    tm = TM_POST
    resident = lambda shape: pl.BlockSpec(shape, lambda r: (0,) * len(shape),
                                          pipeline_mode=pl.Buffered(1))
    return pl.pallas_call(
        _post_kernel,
        grid=(R // tm,),
        in_specs=[
            pl.BlockSpec((tm, DIFF_WIDTH), lambda r: (r, 0)),
            pl.BlockSpec((tm, FOX_WIDTH), lambda r: (r, 0)),
            pl.BlockSpec((tm, D), lambda r: (r, 0)),
            resident(wod.shape), resident(wof.shape), resident(wup.shape), resident(wdn.shape),
            resident((1, D)), resident((1, D)), resident((1, D)),
        ],
        out_specs=pl.BlockSpec((tm, D), lambda r: (r, 0)),
        out_shape=jax.ShapeDtypeStruct((R, D), jnp.float32),
        compiler_params=pltpu.CompilerParams(
            dimension_semantics=("parallel",), vmem_limit_bytes=VMEM_LIMIT),
        name="post",
    )(mix_d, mix_f, x2d, wod, wof, wup, wdn, g1, g2, g3)


def kernel(x, ln_attn_pre, w_in, b_f, lam_q1, lam_k1, lam_q2, lam_k2, subln_g, rel_bias,
           w_out, ln_attn_post, ln_mlp_pre, w_up, w_down, ln_mlp_post):
    B, S, D = x.shape
    depth = w_in.shape[0]
    bf16 = jnp.bfloat16
    bias_tiles, far_bucket = _bias_tiles(rel_bias, S)
    dw, fw = DIFF_WIDTH, FOX_WIDTH
    for l in range(depth):
        lambda_init = 0.8 - 0.6 * math.exp(-0.3 * l)
        w = w_in[l]
        o_dq, o_dk, o_dv, o_fq, o_fk, o_fv, o_g = 0, dw, 2 * dw, 3 * dw, 3 * dw + fw, 3 * dw + 2 * fw, 3 * dw + 3 * fw
        wrow = jnp.concatenate([w[:, o_dk:o_dk + dw], w[:, o_fk:o_fk + fw]], axis=1).astype(bf16)
        wcol = jnp.concatenate([w[:, o_dq:o_dq + dw], w[:, o_dv:o_dv + dw],
                                w[:, o_fq:o_fq + fw], w[:, o_fv:o_fv + fw]], axis=1).T.astype(bf16)
        wf = jnp.zeros((XROWS, D), bf16).at[:FOX_HEADS].set(w[:, o_g:o_g + FOX_HEADS].T.astype(bf16))
        bf_col = jnp.zeros((XROWS, 1), jnp.float32).at[:FOX_HEADS, 0].set(b_f[l].astype(jnp.float32))

        kd, kf, qd, vd, qf, vf, xq = _proj(x, ln_attn_pre[l][None, :], wrow, wcol, wf, bf_col)
        mix_d = _diff_attention(kd, qd, vd, bias_tiles, far_bucket, rel_bias,
                                lam_q1[l][None, :], lam_k1[l][None, :], lam_q2[l][None, :],
                                lam_k2[l][None, :], subln_g[l][None, :], lambda_init)
        mix_f = _fox_attention(kf, qf, xq, vf)
        wo = w_out[l].astype(bf16)
        y = _post(mix_d.reshape(B * S, dw), mix_f.reshape(B * S, fw), x.reshape(B * S, D),
                  wo[:dw], wo[dw:], w_up[l].astype(bf16), w_down[l].astype(bf16),
                  ln_attn_post[l][None, :], ln_mlp_pre[l][None, :], ln_mlp_post[l][None, :])
        x = y.reshape(B, S, D)
    return x
```

```python
import functools
import math

import numpy as np
import jax
import jax.numpy as jnp
from jax import lax
from jax.experimental import pallas as pl
from jax.experimental.pallas import tpu as pltpu

D_MODEL = 1024
DIFF_HEADS = 4
DIFF_HEAD_DIM = 64
DIFF_WIDTH = DIFF_HEADS * 2 * DIFF_HEAD_DIM
FOX_HEADS = 8
FOX_HEAD_DIM = 64
FOX_WIDTH = FOX_HEADS * FOX_HEAD_DIM
FOX_UNITS = FOX_HEADS // 2
D_FF = 4 * D_MODEL
N_BUCKETS = 32
MAX_DISTANCE = 128
EPS = 1e-6
NEG = -1e30
QK_SCALE = DIFF_HEAD_DIM ** -0.5

TQ = 256
TK = 256
TM_PROJ = 512
TM_POST = 512
FF_CHUNK = 1024
LOOKAHEAD = 3
XROWS = 16
VMEM_LIMIT = 56 * 1024 * 1024


def _rms(x, g):
    ms = jnp.mean(x * x, axis=-1, keepdims=True)
    return x * lax.rsqrt(ms + EPS) * g


def _proj_kernel(x_ref, g_ref, wrow_ref, wcol_ref, wf_ref, bf_ref, pq_ref, pk_ref, oq_ref, ok_ref,
                 kd_ref, kf_ref, qd_ref, vd_ref, qf_ref, vf_ref, xq_ref, carry_ref):
    tm = x_ref.shape[1]
    nsub = tm // TQ
    h = _rms(x_ref[0], g_ref[...]).astype(jnp.bfloat16)

    krow = jnp.dot(h, wrow_ref[...], preferred_element_type=jnp.float32)
    kd_ref[0] = krow[:, :DIFF_WIDTH].astype(jnp.bfloat16)

    nt = (((1,), (1,)), ((), ()))
    outs = (qd_ref, vd_ref, qf_ref, vf_ref)
    scales = (QK_SCALE, None, QK_SCALE, None)
    for n, (o_ref, sc) in enumerate(zip(outs, scales)):
        colt = lax.dot_general(wcol_ref[n * 512:(n + 1) * 512, :], h, nt,
                               preferred_element_type=jnp.float32)
        if sc is not None:
            colt = colt * sc
        colt = colt.astype(jnp.bfloat16)
        for c in range(nsub):
            o_ref[0, c] = colt[:, c * TQ:(c + 1) * TQ]

    ft = lax.dot_general(wf_ref[...], h, nt, preferred_element_type=jnp.float32) + bf_ref[...]
    ls = jnp.minimum(ft, 0.0) - jnp.log1p(jnp.exp(-jnp.abs(ft)))
    lane = lax.broadcasted_iota(jnp.int32, ls.shape, 1)
    sh = 1
    while sh < tm:
        ls = ls + jnp.where(lane >= sh, pltpu.roll(ls, sh, axis=1), 0.0)
        sh *= 2

    @pl.when(pl.program_id(1) == 0)
    def _():
        carry_ref[...] = jnp.zeros_like(carry_ref)

    cum = ls + carry_ref[...]
    carry_ref[...] = cum[:, tm - 1:tm]

    hi = cum.astype(jnp.bfloat16)
    r1 = cum - hi.astype(jnp.float32)
    mid = r1.astype(jnp.bfloat16)
    lo = (r1 - mid.astype(jnp.float32)).astype(jnp.bfloat16)
    parts = jnp.concatenate([hi, mid, lo, jnp.zeros((128 - 3 * XROWS, tm), jnp.bfloat16)], axis=0)
    xq = jnp.dot(pq_ref[...], parts, preferred_element_type=jnp.float32) + oq_ref[...]
    xq = xq.astype(jnp.bfloat16)
    for c in range(nsub):
        xq_ref[0, c] = xq[:, c * TQ:(c + 1) * TQ]
    xk = jnp.dot(pk_ref[...], parts, preferred_element_type=jnp.float32) + ok_ref[...]
    for u in range(FOX_UNITS):
        kf_ref[0, :, 256 * u:256 * u + 128] = (
            krow[:, DIFF_WIDTH + 128 * u:DIFF_WIDTH + 128 * (u + 1)].astype(jnp.bfloat16))
        kf_ref[0, :, 256 * u + 128:256 * (u + 1)] = (
            xk[128 * u:128 * (u + 1), :].T.astype(jnp.bfloat16))


def _placement_constants():
    pq = np.zeros((FOX_UNITS * 2 * XROWS, 128), np.float32)
    oq = np.zeros((FOX_UNITS * 2 * XROWS, 1), np.float32)
    pk = np.zeros((FOX_UNITS * 128, 128), np.float32)
    ok = np.zeros((FOX_UNITS * 128, 1), np.float32)
    for u in range(FOX_UNITS):
        for a in range(2):
            head = 2 * u + a
            for r in range(3):
                pq[2 * XROWS * u + XROWS * a + r, XROWS * r + head] = 1.0
                oq[2 * XROWS * u + XROWS * a + 3 + r, 0] = 1.0
                pk[128 * u + XROWS * a + 3 + r, XROWS * r + head] = -1.0
                ok[128 * u + XROWS * a + r, 0] = 1.0
    return (jnp.asarray(pq, jnp.bfloat16), jnp.asarray(pk, jnp.bfloat16),
            jnp.asarray(oq), jnp.asarray(ok))


def _proj(x, g, wrow, wcol, wf, bf_col):
    B, S, D = x.shape
    tm = TM_PROJ
    nsub = tm // TQ
    nq = S // TQ
    pq, pk, oq, ok = _placement_constants()
    const = lambda shape: pl.BlockSpec(shape, lambda b, t: (0,) * len(shape))
    colspec = lambda rows: pl.BlockSpec((1, nsub, rows, TQ), lambda b, t: (b, t, 0, 0))
    bf16 = jnp.bfloat16
    return pl.pallas_call(
        _proj_kernel,
        grid=(B, S // tm),
        in_specs=[
            pl.BlockSpec((1, tm, D), lambda b, t: (b, t, 0)),
            const((1, D)), const(wrow.shape), const(wcol.shape), const(wf.shape), const(bf_col.shape),
            const(pq.shape), const(pk.shape), const(oq.shape), const(ok.shape),
        ],
        out_specs=[
            pl.BlockSpec((1, tm, DIFF_WIDTH), lambda b, t: (b, t, 0)),
            pl.BlockSpec((1, tm, 2 * FOX_WIDTH), lambda b, t: (b, t, 0)),
            colspec(512), colspec(512), colspec(512), colspec(512), colspec(FOX_UNITS * 2 * XROWS),
        ],
        out_shape=[
            jax.ShapeDtypeStruct((B, S, DIFF_WIDTH), bf16),
            jax.ShapeDtypeStruct((B, S, 2 * FOX_WIDTH), bf16),
            jax.ShapeDtypeStruct((B, nq, 512, TQ), bf16),
            jax.ShapeDtypeStruct((B, nq, 512, TQ), bf16),
            jax.ShapeDtypeStruct((B, nq, 512, TQ), bf16),
            jax.ShapeDtypeStruct((B, nq, 512, TQ), bf16),
            jax.ShapeDtypeStruct((B, nq, FOX_UNITS * 2 * XROWS, TQ), bf16),
        ],
        scratch_shapes=[pltpu.VMEM((XROWS, 1), jnp.float32)],
        compiler_params=pltpu.CompilerParams(
            dimension_semantics=("arbitrary", "arbitrary"), vmem_limit_bytes=VMEM_LIMIT),
        name="proj",
    )(x, g, wrow, wcol, wf, bf_col, pq, pk, oq, ok)


def _t5_buckets(n):
    max_exact = N_BUCKETS // 2
    dist = np.arange(n)
    d = np.maximum(dist, 1).astype(np.float32)
    large = np.float32(max_exact) + (np.log(d / np.float32(max_exact))
                                     / np.float32(math.log(MAX_DISTANCE / max_exact))
                                     * np.float32(N_BUCKETS - max_exact))
    large = np.minimum(large.astype(np.int32), N_BUCKETS - 1)
    return np.where(dist < max_exact, dist, large).astype(np.int32)


def _bias_kernel(rel_ref, idx_ref, out_ref):
    hd = pl.program_id(0)
    for t in range(2):
        idx = idx_ref[t]
        val = jnp.full(idx.shape, NEG, jnp.float32)
        for bkt in range(N_BUCKETS):
            val = jnp.where(idx == bkt, rel_ref[bkt, hd], val)
        out_ref[0, t] = val


def _bias_tiles(rel_bias, S):
    buckets = _t5_buckets(S)
    far = int(buckets[-1])
    assert np.all(buckets[TQ + 1:] == far)
    s_loc = np.arange(TK)[:, None]
    t_loc = np.arange(TQ)[None, :]
    rel0 = t_loc - s_loc
    idx0 = np.where(rel0 >= 0, buckets[np.maximum(rel0, 0)], -1)
    idx1 = buckets[TQ + t_loc - s_loc]
    idx = jnp.asarray(np.stack([idx0, idx1]).astype(np.int32))
    tiles = pl.pallas_call(
        _bias_kernel,
        grid=(DIFF_HEADS,),
        in_specs=[pl.BlockSpec(memory_space=pltpu.SMEM),
                  pl.BlockSpec((2, TK, TQ), lambda h: (0, 0, 0))],
        out_specs=pl.BlockSpec((1, 2, TK, TQ), lambda h: (h, 0, 0, 0)),
        out_shape=jax.ShapeDtypeStruct((DIFF_HEADS, 2, TK, TQ), jnp.float32),
        name="t5_bias_tiles",
    )(rel_bias, idx)
    return tiles, far


def _flash_update(m_ref, l_ref, acc_ref, a, st, vt, shift=None):
    m_old = m_ref[a]
    mx = jnp.max(st, axis=0, keepdims=True)
    if shift is None:
        m_new = jnp.maximum(m_old, mx)
        p = jnp.exp(st - m_new)
    else:
        m_new = jnp.maximum(m_old, mx + shift)
        p = jnp.exp(st - (m_new - shift))
    alpha = jnp.exp(m_old - m_new)
    l_ref[a] = alpha * l_ref[a] + jnp.sum(p, axis=0, keepdims=True)
    acc_ref[a] = alpha * acc_ref[a] + jnp.dot(vt, p.astype(jnp.bfloat16),
                                              preferred_element_type=jnp.float32)
    m_ref[a] = m_new


def _flash_init(m_ref, l_ref, acc_ref):
    m_ref[...] = jnp.full(m_ref.shape, NEG, jnp.float32)
    l_ref[...] = jnp.zeros_like(l_ref)
    acc_ref[...] = jnp.zeros_like(acc_ref)


def _run_chains(n, score_fn, update_fn, lookahead):
    pending = []
    for c in range(n):
        pending.append((c, score_fn(c)))
        if len(pending) > lookahead:
            update_fn(*pending.pop(0))
    for item in pending:
        update_fn(*item)


def _diff_kernel(far_bucket, lambda_init,
                 rel_ref, k_ref, q_ref, v_ref, bias_ref, lq1_ref, lk1_ref, lq2_ref, lk2_ref, g_ref,
                 o_ref, qpad_ref, m_ref, l_ref, acc_ref):
    i = pl.program_id(1)
    d = DIFF_HEAD_DIM
    qpad_ref[...] = jnp.zeros_like(qpad_ref)
    for c in range(2 * DIFF_HEADS):
        r0 = d * (c % 4)
        qpad_ref[c, r0:r0 + d] = q_ref[0, 0, d * c:d * (c + 1)]
    _flash_init(m_ref, l_ref, acc_ref)

    def step(j, tile):
        row0 = pl.multiple_of(j * TK, TK)

        def score(c):
            pair = c // 4
            kblk = k_ref[0, pl.ds(row0, TK), 256 * pair:256 * (pair + 1)]
            return jnp.dot(kblk, qpad_ref[c], preferred_element_type=jnp.float32)

        def update(c, st):
            hd = c // 2
            vt = v_ref[0, j, 128 * hd:128 * (hd + 1)]
            if tile is None:
                _flash_update(m_ref, l_ref, acc_ref, c, st, vt, shift=rel_ref[far_bucket, hd])
            else:
                _flash_update(m_ref, l_ref, acc_ref, c, bias_ref[hd, tile] + st, vt)

        _run_chains(2 * DIFF_HEADS, score, update, LOOKAHEAD)

    def far_body(j, carry):
        step(j, None)
        return carry

    lax.fori_loop(0, jnp.maximum(i - 1, 0), far_body, 0)

    @pl.when(i >= 1)
    def _():
        step(i - 1, 1)

    step(i, 0)

    lam = (jnp.exp(jnp.sum(lq1_ref[...] * lk1_ref[...], axis=-1, keepdims=True))
           - jnp.exp(jnp.sum(lq2_ref[...] * lk2_ref[...], axis=-1, keepdims=True))
           + lambda_init)
    for hd in range(DIFF_HEADS):
        o0 = acc_ref[2 * hd] * (1.0 / l_ref[2 * hd])
        o1 = acc_ref[2 * hd + 1] * (1.0 / l_ref[2 * hd + 1])
        od = o0 - lam * o1
        ms = jnp.mean(od * od, axis=0, keepdims=True)
        y = (od * lax.rsqrt(ms + EPS)).T * g_ref[...]
        o_ref[0, :, 128 * hd:128 * (hd + 1)] = (y * (1.0 - lambda_init)).astype(o_ref.dtype)


def _diff_attention(kd, qd, vd, bias_tiles, far_bucket, rel_bias, lq1, lk1, lq2, lk2, g, lambda_init):
    B, S, _ = kd.shape
    nq = S // TQ
    nmaps = 2 * DIFF_HEADS
    vec = lambda n: pl.BlockSpec((1, n), lambda b, i: (0, 0))
    return pl.pallas_call(
        functools.partial(_diff_kernel, far_bucket, lambda_init),
        grid=(B, nq),
        in_specs=[
            pl.BlockSpec(memory_space=pltpu.SMEM),
            pl.BlockSpec((1, S, DIFF_WIDTH), lambda b, i: (b, 0, 0)),
            pl.BlockSpec((1, 1, DIFF_WIDTH, TQ), lambda b, i: (b, i, 0, 0)),
            pl.BlockSpec((1, nq, DIFF_WIDTH, TK), lambda b, i: (b, 0, 0, 0)),
            pl.BlockSpec((DIFF_HEADS, 2, TK, TQ), lambda b, i: (0, 0, 0, 0)),
            vec(DIFF_HEAD_DIM), vec(DIFF_HEAD_DIM), vec(DIFF_HEAD_DIM), vec(DIFF_HEAD_DIM),
            vec(2 * DIFF_HEAD_DIM),
        ],
        out_specs=pl.BlockSpec((1, TQ, DIFF_WIDTH), lambda b, i: (b, i, 0)),
        out_shape=jax.ShapeDtypeStruct((B, S, DIFF_WIDTH), jnp.bfloat16),
        scratch_shapes=[
            pltpu.VMEM((nmaps, 256, TQ), jnp.bfloat16),
            pltpu.VMEM((nmaps, 1, TQ), jnp.float32),
            pltpu.VMEM((nmaps, 1, TQ), jnp.float32),
            pltpu.VMEM((nmaps, 128, TQ), jnp.float32),
        ],
        compiler_params=pltpu.CompilerParams(
            dimension_semantics=("parallel", "arbitrary"), vmem_limit_bytes=VMEM_LIMIT),
        name="diff_attention",
    )(rel_bias, kd, qd, vd, bias_tiles, lq1, lk1, lq2, lk2, g)


def _fox_kernel(k_ref, q_ref, xq_ref, v_ref, o_ref, qpad_ref, m_ref, l_ref, acc_ref):
    i = pl.program_id(1)
    d = FOX_HEAD_DIM
    qpad_ref[...] = jnp.zeros_like(qpad_ref)
    for hd in range(FOX_HEADS):
        a = hd % 2
        qpad_ref[hd, d * a:d * (a + 1)] = q_ref[0, 0, d * hd:d * (hd + 1)]
        qpad_ref[hd, 128 + XROWS * a:128 + XROWS * (a + 1)] = xq_ref[0, 0, XROWS * hd:XROWS * (hd + 1)]
    _flash_init(m_ref, l_ref, acc_ref)

    def step(j, masked):
        row0 = pl.multiple_of(j * TK, TK)

        def score(hd):
            u = hd // 2
            kblk = k_ref[0, pl.ds(row0, TK), 256 * u:256 * (u + 1)]
            return jnp.dot(kblk, qpad_ref[hd], preferred_element_type=jnp.float32)

        def update(hd, st):
            if masked:
                row = lax.broadcasted_iota(jnp.int32, st.shape, 0)
                col = lax.broadcasted_iota(jnp.int32, st.shape, 1)
                st = jnp.where(row > col, NEG, st)
            _flash_update(m_ref, l_ref, acc_ref, hd, st, v_ref[0, j, d * hd:d * (hd + 1)])

        _run_chains(FOX_HEADS, score, update, LOOKAHEAD)

    def body(j, carry):
        step(j, False)
        return carry

    lax.fori_loop(0, i, body, 0)
    step(i, True)

    for u in range(FOX_UNITS):
        o = jnp.concatenate([acc_ref[2 * u] * (1.0 / l_ref[2 * u]),
                             acc_ref[2 * u + 1] * (1.0 / l_ref[2 * u + 1])], axis=0)
        o_ref[0, :, 128 * u:128 * (u + 1)] = o.T.astype(o_ref.dtype)


def _fox_attention(kf, qf, xq, vf):
    B, S, _ = kf.shape
    nq = S // TQ
    return pl.pallas_call(
        _fox_kernel,
        grid=(B, nq),
        in_specs=[
            pl.BlockSpec((1, S, 2 * FOX_WIDTH), lambda b, i: (b, 0, 0)),
            pl.BlockSpec((1, 1, FOX_WIDTH, TQ), lambda b, i: (b, i, 0, 0)),
            pl.BlockSpec((1, 1, FOX_HEADS * XROWS, TQ), lambda b, i: (b, i, 0, 0)),
            pl.BlockSpec((1, nq, FOX_WIDTH, TK), lambda b, i: (b, 0, 0, 0)),
        ],
        out_specs=pl.BlockSpec((1, TQ, FOX_WIDTH), lambda b, i: (b, i, 0)),
        out_shape=jax.ShapeDtypeStruct((B, S, FOX_WIDTH), jnp.bfloat16),
        scratch_shapes=[
            pltpu.VMEM((FOX_HEADS, 256, TQ), jnp.bfloat16),
            pltpu.VMEM((FOX_HEADS, 1, TQ), jnp.float32),
            pltpu.VMEM((FOX_HEADS, 1, TQ), jnp.float32),
            pltpu.VMEM((FOX_HEADS, FOX_HEAD_DIM, TQ), jnp.float32),
        ],
        compiler_params=pltpu.CompilerParams(
            dimension_semantics=("parallel", "arbitrary"), vmem_limit_bytes=VMEM_LIMIT),
        name="fox_attention",
    )(kf, qf, xq, vf)


def _post_kernel(md_ref, mf_ref, x_ref, wod_ref, wof_ref, wup_ref, wdn_ref, g1_ref, g2_ref, g3_ref,
                 o_ref):
    a = (jnp.dot(md_ref[...], wod_ref[...], preferred_element_type=jnp.float32)
         + jnp.dot(mf_ref[...], wof_ref[...], preferred_element_type=jnp.float32))
    x1 = x_ref[...] + _rms(a, g1_ref[...])
    h2 = _rms(x1, g2_ref[...]).astype(jnp.bfloat16)
    acc = None
    for c in range(D_FF // FF_CHUNK):
        u = jnp.dot(h2, wup_ref[:, c * FF_CHUNK:(c + 1) * FF_CHUNK], preferred_element_type=jnp.float32)
        u = jnp.square(jnp.maximum(u, 0.0)).astype(jnp.bfloat16)
        part = jnp.dot(u, wdn_ref[c * FF_CHUNK:(c + 1) * FF_CHUNK, :], preferred_element_type=jnp.float32)
        acc = part if acc is None else acc + part
    o_ref[...] = x1 + _rms(acc, g3_ref[...])


def _post(mix_d, mix_f, x2d, wod, wof, wup, wdn, g1, g2, g3):
    R, D = x2d.shape
    tm = TM_POST
    resident = lambda shape: pl.BlockSpec(shape, lambda r: (0,) * len(shape),
                                          pipeline_mode=pl.Buffered(1))
    return pl.pallas_call(
        _post_kernel,
        grid=(R // tm,),
        in_specs=[
            pl.BlockSpec((tm, DIFF_WIDTH), lambda r: (r, 0)),
            pl.BlockSpec((tm, FOX_WIDTH), lambda r: (r, 0)),
            pl.BlockSpec((tm, D), lambda r: (r, 0)),
            resident(wod.shape), resident(wof.shape), resident(wup.shape), resident(wdn.shape),
            resident((1, D)), resident((1, D)), resident((1, D)),
        ],
        out_specs=pl.BlockSpec((tm, D), lambda r: (r, 0)),
        out_shape=jax.ShapeDtypeStruct((R, D), jnp.float32),
        compiler_params=pltpu.CompilerParams(
            dimension_semantics=("parallel",), vmem_limit_bytes=VMEM_LIMIT),
        name="post",
    )(mix_d, mix_f, x2d, wod, wof, wup, wdn, g1, g2, g3)


def kernel(x, ln_attn_pre, w_in, b_f, lam_q1, lam_k1, lam_q2, lam_k2, subln_g, rel_bias,
           w_out, ln_attn_post, ln_mlp_pre, w_up, w_down, ln_mlp_post):
    B, S, D = x.shape
    depth = w_in.shape[0]
    bf16 = jnp.bfloat16
    bias_tiles, far_bucket = _bias_tiles(rel_bias, S)
    dw, fw = DIFF_WIDTH, FOX_WIDTH
    for l in range(depth):
        lambda_init = 0.8 - 0.6 * math.exp(-0.3 * l)
        w = w_in[l]
        o_dq, o_dk, o_dv, o_fq, o_fk, o_fv, o_g = 0, dw, 2 * dw, 3 * dw, 3 * dw + fw, 3 * dw + 2 * fw, 3 * dw + 3 * fw
        wrow = jnp.concatenate([w[:, o_dk:o_dk + dw], w[:, o_fk:o_fk + fw]], axis=1).astype(bf16)
        wcol = jnp.concatenate([w[:, o_dq:o_dq + dw], w[:, o_dv:o_dv + dw],
                                w[:, o_fq:o_fq + fw], w[:, o_fv:o_fv + fw]], axis=1).T.astype(bf16)
        wf = jnp.zeros((XROWS, D), bf16).at[:FOX_HEADS].set(w[:, o_g:o_g + FOX_HEADS].T.astype(bf16))
        bf_col = jnp.zeros((XROWS, 1), jnp.float32).at[:FOX_HEADS, 0].set(b_f[l].astype(jnp.float32))

        kd, kf, qd, vd, qf, vf, xq = _proj(x, ln_attn_pre[l][None, :], wrow, wcol, wf, bf_col)
        mix_d = _diff_attention(kd, qd, vd, bias_tiles, far_bucket, rel_bias,
                                lam_q1[l][None, :], lam_k1[l][None, :], lam_q2[l][None, :],
                                lam_k2[l][None, :], subln_g[l][None, :], lambda_init)
        mix_f = _fox_attention(kf, qf, xq, vf)
        wo = w_out[l].astype(bf16)
        y = _post(mix_d.reshape(B * S, dw), mix_f.reshape(B * S, fw), x.reshape(B * S, D),
                  wo[:dw], wo[dw:], w_up[l].astype(bf16), w_down[l].astype(bf16),
                  ln_attn_post[l][None, :], ln_mlp_pre[l][None, :], ln_mlp_post[l][None, :])
        x = y.reshape(B, S, D)
    return x
```

```python
import functools
import math

import numpy as np
import jax
import jax.numpy as jnp
from jax import lax
from jax.experimental import pallas as pl
from jax.experimental.pallas import tpu as pltpu

D_MODEL = 1024
DIFF_HEADS = 4
DIFF_HEAD_DIM = 64
DIFF_WIDTH = DIFF_HEADS * 2 * DIFF_HEAD_DIM
FOX_HEADS = 8
FOX_HEAD_DIM = 64
FOX_WIDTH = FOX_HEADS * FOX_HEAD_DIM
FOX_UNITS = FOX_HEADS // 2
D_FF = 4 * D_MODEL
N_BUCKETS = 32
MAX_DISTANCE = 128
EPS = 1e-6
NEG = -1e30
QK_SCALE = DIFF_HEAD_DIM ** -0.5
LOG2E = math.log2(math.e)

TQ = 256
TK = 256
TM_PROJ = 512
TM_POST = 512
FF_CHUNK = 1024
LOOKAHEAD = 4
ONES_ROWS = 16
DV_ROWS = 2 * DIFF_HEAD_DIM + ONES_ROWS
FV_ROWS = FOX_HEAD_DIM + ONES_ROWS
XROWS = 16
VMEM_LIMIT = 56 * 1024 * 1024


def _rms(x, g):
    ms = jnp.mean(x * x, axis=-1, keepdims=True)
    return x * lax.rsqrt(ms + EPS) * g


def _proj_kernel(x_ref, g_ref, wrow_ref, wcol_ref, wf_ref, bf_ref, pq_ref, pk_ref, oq_ref, ok_ref,
                 kd_ref, kf_ref, qd_ref, vd_ref, qf_ref, vf_ref, xq_ref, carry_ref):
    tm = x_ref.shape[1]
    nsub = tm // TQ
    h = _rms(x_ref[0], g_ref[...]).astype(jnp.bfloat16)

    krow = jnp.dot(h, wrow_ref[...], preferred_element_type=jnp.float32)
    kd_ref[0] = krow[:, :DIFF_WIDTH].astype(jnp.bfloat16)

    nt = (((1,), (1,)), ((), ()))
    ones = jnp.ones((ONES_ROWS, TQ), jnp.bfloat16)
    outs = ((qd_ref, None, None), (vd_ref, 2 * DIFF_HEAD_DIM, DV_ROWS),
            (qf_ref, None, None), (vf_ref, FOX_HEAD_DIM, FV_ROWS))
    for n, (o_ref, hrows, orows) in enumerate(outs):
        colt = lax.dot_general(wcol_ref[n * 512:(n + 1) * 512, :], h, nt,
                               preferred_element_type=jnp.float32)
        if hrows is None:
            colt = (colt * (QK_SCALE * LOG2E)).astype(jnp.bfloat16)
            for c in range(nsub):
                o_ref[0, c] = colt[:, c * TQ:(c + 1) * TQ]
        else:
            colt = colt.astype(jnp.bfloat16)
            for c in range(nsub):
                for hd in range(512 // hrows):
                    o_ref[0, c, orows * hd:orows * hd + hrows] = (
                        colt[hrows * hd:hrows * (hd + 1), c * TQ:(c + 1) * TQ])
                    o_ref[0, c, orows * hd + hrows:orows * (hd + 1)] = ones

    ft = lax.dot_general(wf_ref[...], h, nt, preferred_element_type=jnp.float32) + bf_ref[...]
    ls = jnp.minimum(ft, 0.0) - jnp.log1p(jnp.exp(-jnp.abs(ft)))
    lane = lax.broadcasted_iota(jnp.int32, ls.shape, 1)
    sh = 1
    while sh < tm:
        ls = ls + jnp.where(lane >= sh, pltpu.roll(ls, sh, axis=1), 0.0)
        sh *= 2

    @pl.when(pl.program_id(1) == 0)
    def _():
        carry_ref[...] = jnp.zeros_like(carry_ref)

    cum = ls + carry_ref[...]
    carry_ref[...] = cum[:, tm - 1:tm]
    cum = cum * LOG2E

    hi = cum.astype(jnp.bfloat16)
    r1 = cum - hi.astype(jnp.float32)
    mid = r1.astype(jnp.bfloat16)
    lo = (r1 - mid.astype(jnp.float32)).astype(jnp.bfloat16)
    parts = jnp.concatenate([hi, mid, lo, jnp.zeros((128 - 3 * XROWS, tm), jnp.bfloat16)], axis=0)
    xq = jnp.dot(pq_ref[...], parts, preferred_element_type=jnp.float32) + oq_ref[...]
    xq = xq.astype(jnp.bfloat16)
    for c in range(nsub):
        xq_ref[0, c] = xq[:, c * TQ:(c + 1) * TQ]
    xk = jnp.dot(pk_ref[...], parts, preferred_element_type=jnp.float32) + ok_ref[...]
    for u in range(FOX_UNITS):
        kf_ref[0, :, 256 * u:256 * u + 128] = (
            krow[:, DIFF_WIDTH + 128 * u:DIFF_WIDTH + 128 * (u + 1)].astype(jnp.bfloat16))
        kf_ref[0, :, 256 * u + 128:256 * (u + 1)] = (
            xk[128 * u:128 * (u + 1), :].T.astype(jnp.bfloat16))


def _placement_constants():
    pq = np.zeros((FOX_UNITS * 2 * XROWS, 128), np.float32)
    oq = np.zeros((FOX_UNITS * 2 * XROWS, 1), np.float32)
    pk = np.zeros((FOX_UNITS * 128, 128), np.float32)
    ok = np.zeros((FOX_UNITS * 128, 1), np.float32)
    for u in range(FOX_UNITS):
        for a in range(2):
            head = 2 * u + a
            for r in range(3):
                pq[2 * XROWS * u + XROWS * a + r, XROWS * r + head] = 1.0
                oq[2 * XROWS * u + XROWS * a + 3 + r, 0] = 1.0
                pk[128 * u + XROWS * a + 3 + r, XROWS * r + head] = -1.0
                ok[128 * u + XROWS * a + r, 0] = 1.0
    return (jnp.asarray(pq, jnp.bfloat16), jnp.asarray(pk, jnp.bfloat16),
            jnp.asarray(oq), jnp.asarray(ok))


def _proj(x, g, wrow, wcol, wf, bf_col):
    B, S, D = x.shape
    tm = TM_PROJ
    nsub = tm // TQ
    nq = S // TQ
    pq, pk, oq, ok = _placement_constants()
    const = lambda shape: pl.BlockSpec(shape, lambda b, t: (0,) * len(shape))
    colspec = lambda rows: pl.BlockSpec((1, nsub, rows, TQ), lambda b, t: (b, t, 0, 0))
    bf16 = jnp.bfloat16
    return pl.pallas_call(
        _proj_kernel,
        grid=(B, S // tm),
        in_specs=[
            pl.BlockSpec((1, tm, D), lambda b, t: (b, t, 0)),
            const((1, D)), const(wrow.shape), const(wcol.shape), const(wf.shape), const(bf_col.shape),
            const(pq.shape), const(pk.shape), const(oq.shape), const(ok.shape),
        ],
        out_specs=[
            pl.BlockSpec((1, tm, DIFF_WIDTH), lambda b, t: (b, t, 0)),
            pl.BlockSpec((1, tm, 2 * FOX_WIDTH), lambda b, t: (b, t, 0)),
            colspec(512), colspec(DIFF_HEADS * DV_ROWS), colspec(512), colspec(FOX_HEADS * FV_ROWS),
            colspec(FOX_UNITS * 2 * XROWS),
        ],
        out_shape=[
            jax.ShapeDtypeStruct((B, S, DIFF_WIDTH), bf16),
            jax.ShapeDtypeStruct((B, S, 2 * FOX_WIDTH), bf16),
            jax.ShapeDtypeStruct((B, nq, 512, TQ), bf16),
            jax.ShapeDtypeStruct((B, nq, DIFF_HEADS * DV_ROWS, TQ), bf16),
            jax.ShapeDtypeStruct((B, nq, 512, TQ), bf16),
            jax.ShapeDtypeStruct((B, nq, FOX_HEADS * FV_ROWS, TQ), bf16),
            jax.ShapeDtypeStruct((B, nq, FOX_UNITS * 2 * XROWS, TQ), bf16),
        ],
        scratch_shapes=[pltpu.VMEM((XROWS, 1), jnp.float32)],
        compiler_params=pltpu.CompilerParams(
            dimension_semantics=("arbitrary", "arbitrary"), vmem_limit_bytes=VMEM_LIMIT),
        name="proj",
    )(x, g, wrow, wcol, wf, bf_col, pq, pk, oq, ok)


def _t5_buckets(n):
    max_exact = N_BUCKETS // 2
    dist = np.arange(n)
    d = np.maximum(dist, 1).astype(np.float32)
    large = np.float32(max_exact) + (np.log(d / np.float32(max_exact))
                                     / np.float32(math.log(MAX_DISTANCE / max_exact))
                                     * np.float32(N_BUCKETS - max_exact))
    large = np.minimum(large.astype(np.int32), N_BUCKETS - 1)
    return np.where(dist < max_exact, dist, large).astype(np.int32)


def _bias_kernel(rel_ref, idx_ref, out_ref):
    hd = pl.program_id(0)
    for t in range(2):
        idx = idx_ref[t]
        val = jnp.full(idx.shape, NEG, jnp.float32)
        for bkt in range(N_BUCKETS):
            val = jnp.where(idx == bkt, rel_ref[bkt, hd] * LOG2E, val)
        out_ref[0, t] = val


def _bias_tiles(rel_bias, S):
    buckets = _t5_buckets(S)
    far = int(buckets[-1])
    assert np.all(buckets[TQ + 1:] == far)
    s_loc = np.arange(TK)[:, None]
    t_loc = np.arange(TQ)[None, :]
    rel0 = t_loc - s_loc
    idx0 = np.where(rel0 >= 0, buckets[np.maximum(rel0, 0)], -1)
    idx1 = buckets[TQ + t_loc - s_loc]
    idx = jnp.asarray(np.stack([idx0, idx1]).astype(np.int32))
    tiles = pl.pallas_call(
        _bias_kernel,
        grid=(DIFF_HEADS,),
        in_specs=[pl.BlockSpec(memory_space=pltpu.SMEM),
                  pl.BlockSpec((2, TK, TQ), lambda h: (0, 0, 0))],
        out_specs=pl.BlockSpec((1, 2, TK, TQ), lambda h: (h, 0, 0, 0)),
        out_shape=jax.ShapeDtypeStruct((DIFF_HEADS, 2, TK, TQ), jnp.float32),
        name="t5_bias_tiles",
    )(rel_bias, idx)
    return tiles, far


def _flash_update(m_ref, acc_ref, a, st, vt, shift=None):
    m_old = m_ref[a]
    mx = jnp.max(st, axis=0, keepdims=True)
    if shift is None:
        m_new = jnp.maximum(m_old, mx)
        x = st - m_new
    else:
        m_new = jnp.maximum(m_old, mx + shift)
        x = st - (m_new - shift)
    p = jnp.exp2(x.astype(jnp.bfloat16))
    alpha = jnp.exp2(m_old - m_new)
    acc_ref[a] = alpha * acc_ref[a] + jnp.dot(vt, p, preferred_element_type=jnp.float32)
    m_ref[a] = m_new


def _flash_init(m_ref, acc_ref):
    m_ref[...] = jnp.full(m_ref.shape, NEG, jnp.float32)
    acc_ref[...] = jnp.zeros_like(acc_ref)


def _run_chains(n, st_ref, score_fn, update_fn, j, prefetch):
    pending = [(c, st_ref[c]) for c in range(LOOKAHEAD)]
    for c in range(LOOKAHEAD, n):
        pending.append((c, score_fn(c, j)))
        update_fn(*pending.pop(0))
    for c in range(LOOKAHEAD):
        if prefetch:
            st_ref[c] = score_fn(c, j + 1)
        update_fn(*pending.pop(0))


def _prime_chains(st_ref, score_fn):
    for c in range(LOOKAHEAD):
        st_ref[c] = score_fn(c, 0)


def _diff_kernel(far_bucket, lambda_init,
                 rel_ref, k_ref, q_ref, v_ref, bias_ref, lq1_ref, lk1_ref, lq2_ref, lk2_ref, g_ref,
                 o_ref, qpad_ref, m_ref, acc_ref, st_ref):
    i = pl.program_id(1)
    d = DIFF_HEAD_DIM
    qpad_ref[...] = jnp.zeros_like(qpad_ref)
    for c in range(2 * DIFF_HEADS):
        r0 = d * (c % 4)
        qpad_ref[c, r0:r0 + d] = q_ref[0, 0, d * c:d * (c + 1)]
    _flash_init(m_ref, acc_ref)

    def score(c, j):
        pair = c // 4
        kblk = k_ref[0, pl.ds(pl.multiple_of(j * TK, TK), TK), 256 * pair:256 * (pair + 1)]
        return jnp.dot(kblk, qpad_ref[c], preferred_element_type=jnp.float32)

    def step(j, tile, prefetch):
        def update(c, st):
            hd = c // 2
            vt = v_ref[0, j, DV_ROWS * hd:DV_ROWS * (hd + 1)]
            if tile is None:
                _flash_update(m_ref, acc_ref, c, st, vt, shift=rel_ref[far_bucket, hd] * LOG2E)
            else:
                _flash_update(m_ref, acc_ref, c, bias_ref[hd, tile] + st, vt)

        _run_chains(2 * DIFF_HEADS, st_ref, score, update, j, prefetch)

    _prime_chains(st_ref, score)

    def far_body(j, carry):
        step(j, None, True)
        return carry

    lax.fori_loop(0, jnp.maximum(i - 1, 0), far_body, 0)

    @pl.when(i >= 1)
    def _():
        step(i - 1, 1, True)

    step(i, 0, False)

    lam = (jnp.exp(jnp.sum(lq1_ref[...] * lk1_ref[...], axis=-1, keepdims=True))
           - jnp.exp(jnp.sum(lq2_ref[...] * lk2_ref[...], axis=-1, keepdims=True))
           + lambda_init)
    for hd in range(DIFF_HEADS):
        dv = 2 * d
        o0 = acc_ref[2 * hd, :dv] * (1.0 / acc_ref[2 * hd, dv:dv + 1])
        o1 = acc_ref[2 * hd + 1, :dv] * (1.0 / acc_ref[2 * hd + 1, dv:dv + 1])
        od = o0 - lam * o1
        ms = jnp.mean(od * od, axis=0, keepdims=True)
        y = (od * lax.rsqrt(ms + EPS)).T * g_ref[...]
        o_ref[0, :, 128 * hd:128 * (hd + 1)] = (y * (1.0 - lambda_init)).astype(o_ref.dtype)


def _diff_attention(kd, qd, vd, bias_tiles, far_bucket, rel_bias, lq1, lk1, lq2, lk2, g, lambda_init):
    B, S, _ = kd.shape
    nq = S // TQ
    nmaps = 2 * DIFF_HEADS
    vec = lambda n: pl.BlockSpec((1, n), lambda b, i: (0, 0))
    return pl.pallas_call(
        functools.partial(_diff_kernel, far_bucket, lambda_init),
        grid=(B, nq),
        in_specs=[
            pl.BlockSpec(memory_space=pltpu.SMEM),
            pl.BlockSpec((1, S, DIFF_WIDTH), lambda b, i: (b, 0, 0)),
            pl.BlockSpec((1, 1, DIFF_WIDTH, TQ), lambda b, i: (b, i, 0, 0)),
            pl.BlockSpec((1, nq, DIFF_HEADS * DV_ROWS, TK), lambda b, i: (b, 0, 0, 0)),
            pl.BlockSpec((DIFF_HEADS, 2, TK, TQ), lambda b, i: (0, 0, 0, 0)),
            vec(DIFF_HEAD_DIM), vec(DIFF_HEAD_DIM), vec(DIFF_HEAD_DIM), vec(DIFF_HEAD_DIM),
            vec(2 * DIFF_HEAD_DIM),
        ],
        out_specs=pl.BlockSpec((1, TQ, DIFF_WIDTH), lambda b, i: (b, i, 0)),
        out_shape=jax.ShapeDtypeStruct((B, S, DIFF_WIDTH), jnp.bfloat16),
        scratch_shapes=[
            pltpu.VMEM((nmaps, 256, TQ), jnp.bfloat16),
            pltpu.VMEM((nmaps, 1, TQ), jnp.float32),
            pltpu.VMEM((nmaps, DV_ROWS, TQ), jnp.float32),
            pltpu.VMEM((LOOKAHEAD, TK, TQ), jnp.float32),
        ],
        compiler_params=pltpu.CompilerParams(
            dimension_semantics=("parallel", "arbitrary"), vmem_limit_bytes=VMEM_LIMIT),
        name="diff_attention",
    )(rel_bias, kd, qd, vd, bias_tiles, lq1, lk1, lq2, lk2, g)


def _fox_kernel(k_ref, q_ref, xq_ref, v_ref, o_ref, qpad_ref, m_ref, acc_ref, st_ref):
    i = pl.program_id(1)
    d = FOX_HEAD_DIM
    qpad_ref[...] = jnp.zeros_like(qpad_ref)
    for hd in range(FOX_HEADS):
        a = hd % 2
        qpad_ref[hd, d * a:d * (a + 1)] = q_ref[0, 0, d * hd:d * (hd + 1)]
        qpad_ref[hd, 128 + XROWS * a:128 + XROWS * (a + 1)] = xq_ref[0, 0, XROWS * hd:XROWS * (hd + 1)]
    _flash_init(m_ref, acc_ref)

    def score(hd, j):
        u = hd // 2
        kblk = k_ref[0, pl.ds(pl.multiple_of(j * TK, TK), TK), 256 * u:256 * (u + 1)]
        return jnp.dot(kblk, qpad_ref[hd], preferred_element_type=jnp.float32)

    def step(j, diagonal):
        def update(hd, st):
            if diagonal:
                row = lax.broadcasted_iota(jnp.int32, st.shape, 0)
                col = lax.broadcasted_iota(jnp.int32, st.shape, 1)
                st = jnp.where(row > col, NEG, st)
            _flash_update(m_ref, acc_ref, hd, st, v_ref[0, j, FV_ROWS * hd:FV_ROWS * (hd + 1)])

        _run_chains(FOX_HEADS, st_ref, score, update, j, not diagonal)

    _prime_chains(st_ref, score)

    def body(j, carry):
        step(j, False)
        return carry

    lax.fori_loop(0, i, body, 0)
    step(i, True)

    for u in range(FOX_UNITS):
        o = jnp.concatenate([acc_ref[hd, :d] * (1.0 / acc_ref[hd, d:d + 1]) for hd in (2 * u, 2 * u + 1)],
                            axis=0)
        o_ref[0, :, 128 * u:128 * (u + 1)] = o.T.astype(o_ref.dtype)


def _fox_attention(kf, qf, xq, vf):
    B, S, _ = kf.shape
    nq = S // TQ
    return pl.pallas_call(
        _fox_kernel,
        grid=(B, nq),
        in_specs=[
            pl.BlockSpec((1, S, 2 * FOX_WIDTH), lambda b, i: (b, 0, 0)),
            pl.BlockSpec((1, 1, FOX_WIDTH, TQ), lambda b, i: (b, i, 0, 0)),
            pl.BlockSpec((1, 1, FOX_HEADS * XROWS, TQ), lambda b, i: (b, i, 0, 0)),
            pl.BlockSpec((1, nq, FOX_HEADS * FV_ROWS, TK), lambda b, i: (b, 0, 0, 0)),
        ],
        out_specs=pl.BlockSpec((1, TQ, FOX_WIDTH), lambda b, i: (b, i, 0)),
        out_shape=jax.ShapeDtypeStruct((B, S, FOX_WIDTH), jnp.bfloat16),
        scratch_shapes=[
            pltpu.VMEM((FOX_HEADS, 256, TQ), jnp.bfloat16),
            pltpu.VMEM((FOX_HEADS, 1, TQ), jnp.float32),
            pltpu.VMEM((FOX_HEADS, FV_ROWS, TQ), jnp.float32),
            pltpu.VMEM((LOOKAHEAD, TK, TQ), jnp.float32),
        ],
        compiler_params=pltpu.CompilerParams(
            dimension_semantics=("parallel", "arbitrary"), vmem_limit_bytes=VMEM_LIMIT),
        name="fox_attention",
    )(kf, qf, xq, vf)


def _post_kernel(md_ref, mf_ref, x_ref, wod_ref, wof_ref, wup_ref, wdn_ref, g1_ref, g2_ref, g3_ref,
                 o_ref):
    a = (jnp.dot(md_ref[...], wod_ref[...], preferred_element_type=jnp.float32)
         + jnp.dot(mf_ref[...], wof_ref[...], preferred_element_type=jnp.float32))
    x1 = x_ref[...] + _rms(a, g1_ref[...])
    h2 = _rms(x1, g2_ref[...]).astype(jnp.bfloat16)
    acc = None
    for c in range(D_FF // FF_CHUNK):
        u = jnp.dot(h2, wup_ref[:, c * FF_CHUNK:(c + 1) * FF_CHUNK], preferred_element_type=jnp.float32)
        u = jnp.square(jnp.maximum(u, 0.0)).astype(jnp.bfloat16)
        part = jnp.dot(u, wdn_ref[c * FF_CHUNK:(c + 1) * FF_CHUNK, :], preferred_element_type=jnp.float32)
        acc = part if acc is None else acc + part
    o_ref[...] = x1 + _rms(acc, g3_ref[...])


def _post(mix_d, mix_f, x2d, wod, wof, wup, wdn, g1, g2, g3):
    R, D = x2d.shape
    tm = TM_POST
    resident = lambda shape: pl.BlockSpec(shape, lambda r: (0,) * len(shape),
                                          pipeline_mode=pl.Buffered(1))
    return pl.pallas_call(
        _post_kernel,
        grid=(R // tm,),
        in_specs=[
            pl.BlockSpec((tm, DIFF_WIDTH), lambda r: (r, 0)),
            pl.BlockSpec((tm, FOX_WIDTH), lambda r: (r, 0)),
            pl.BlockSpec((tm, D), lambda r: (r, 0)),
            resident(wod.shape), resident(wof.shape), resident(wup.shape), resident(wdn.shape),
            resident((1, D)), resident((1, D)), resident((1, D)),
        ],
        out_specs=pl.BlockSpec((tm, D), lambda r: (r, 0)),
        out_shape=jax.ShapeDtypeStruct((R, D), jnp.float32),
        compiler_params=pltpu.CompilerParams(
            dimension_semantics=("parallel",), vmem_limit_bytes=VMEM_LIMIT),
        name="post",
    )(mix_d, mix_f, x2d, wod, wof, wup, wdn, g1, g2, g3)


def kernel(x, ln_attn_pre, w_in, b_f, lam_q1, lam_k1, lam_q2, lam_k2, subln_g, rel_bias,
           w_out, ln_attn_post, ln_mlp_pre, w_up, w_down, ln_mlp_post):
    B, S, D = x.shape
    depth = w_in.shape[0]
    bf16 = jnp.bfloat16
    bias_tiles, far_bucket = _bias_tiles(rel_bias, S)
    dw, fw = DIFF_WIDTH, FOX_WIDTH
    for l in range(depth):
        lambda_init = 0.8 - 0.6 * math.exp(-0.3 * l)
        w = w_in[l]
        o_dq, o_dk, o_dv, o_fq, o_fk, o_fv, o_g = 0, dw, 2 * dw, 3 * dw, 3 * dw + fw, 3 * dw + 2 * fw, 3 * dw + 3 * fw
        wrow = jnp.concatenate([w[:, o_dk:o_dk + dw], w[:, o_fk:o_fk + fw]], axis=1).astype(bf16)
        wcol = jnp.concatenate([w[:, o_dq:o_dq + dw], w[:, o_dv:o_dv + dw],
                                w[:, o_fq:o_fq + fw], w[:, o_fv:o_fv + fw]], axis=1).T.astype(bf16)
        wf = jnp.zeros((XROWS, D), bf16).at[:FOX_HEADS].set(w[:, o_g:o_g + FOX_HEADS].T.astype(bf16))
        bf_col = jnp.zeros((XROWS, 1), jnp.float32).at[:FOX_HEADS, 0].set(b_f[l].astype(jnp.float32))

        kd, kf, qd, vd, qf, vf, xq = _proj(x, ln_attn_pre[l][None, :], wrow, wcol, wf, bf_col)
        mix_d = _diff_attention(kd, qd, vd, bias_tiles, far_bucket, rel_bias,
                                lam_q1[l][None, :], lam_k1[l][None, :], lam_q2[l][None, :],
                                lam_k2[l][None, :], subln_g[l][None, :], lambda_init)
        mix_f = _fox_attention(kf, qf, xq, vf)
        wo = w_out[l].astype(bf16)
        y = _post(mix_d.reshape(B * S, dw), mix_f.reshape(B * S, fw), x.reshape(B * S, D),
                  wo[:dw], wo[dw:], w_up[l].astype(bf16), w_down[l].astype(bf16),
                  ln_attn_post[l][None, :], ln_mlp_pre[l][None, :], ln_mlp_post[l][None, :])
        x = y.reshape(B, S, D)
    return x
```

```python
import functools
import math

import numpy as np
import jax
import jax.numpy as jnp
from jax import lax
from jax.experimental import pallas as pl
from jax.experimental.pallas import tpu as pltpu

D_MODEL = 1024
DIFF_HEADS = 4
DIFF_HEAD_DIM = 64
DIFF_WIDTH = DIFF_HEADS * 2 * DIFF_HEAD_DIM
FOX_HEADS = 8
FOX_HEAD_DIM = 64
FOX_WIDTH = FOX_HEADS * FOX_HEAD_DIM
FOX_UNITS = FOX_HEADS // 2
D_FF = 4 * D_MODEL
N_BUCKETS = 32
MAX_DISTANCE = 128
EPS = 1e-6
NEG = -1e30
QK_SCALE = DIFF_HEAD_DIM ** -0.5
LOG2E = math.log2(math.e)

TQ = 256
TK = 256
TM_PROJ = 512
TM_POST = 512
FF_CHUNK = 1024
LOOKAHEAD = 4
ONES_ROWS = 16
DV_ROWS = 2 * DIFF_HEAD_DIM + ONES_ROWS
FV_ROWS = FOX_HEAD_DIM + ONES_ROWS
XROWS = 16
VMEM_LIMIT = 56 * 1024 * 1024


def _rms(x, g):
    ms = jnp.mean(x * x, axis=-1, keepdims=True)
    return x * lax.rsqrt(ms + EPS) * g


def _proj_kernel(x_ref, g_ref, wrow_ref, wcol_ref, wf_ref, bf_ref, pq_ref, pk_ref, oq_ref, ok_ref,
                 kd_ref, kf_ref, qd_ref, vd_ref, qf_ref, vf_ref, xq_ref, carry_ref):
    tm = x_ref.shape[1]
    nsub = tm // TQ
    nt = (((1,), (1,)), ((), ()))
    ones = jnp.ones((ONES_ROWS, TQ), jnp.bfloat16)
    outs = ((qd_ref, None, None), (vd_ref, 2 * DIFF_HEAD_DIM, DV_ROWS),
            (qf_ref, None, None), (vf_ref, FOX_HEAD_DIM, FV_ROWS))
    def column_outputs(c, h, n):
        o_ref, hrows, orows = outs[n]
        colt = lax.dot_general(wcol_ref[n * 512:(n + 1) * 512, :], h, nt,
                               preferred_element_type=jnp.float32)
        if hrows is None:
            o_ref[0, c] = (colt * (QK_SCALE * LOG2E)).astype(jnp.bfloat16)
        else:
            colt = colt.astype(jnp.bfloat16)
            for hd in range(512 // hrows):
                o_ref[0, c, orows * hd:orows * hd + hrows] = colt[hrows * hd:hrows * (hd + 1)]
                o_ref[0, c, orows * hd + hrows:orows * (hd + 1)] = ones

    ft_parts = []
    for c in range(nsub):
        rows = slice(c * TQ, (c + 1) * TQ)
        h = _rms(x_ref[0, rows], g_ref[...]).astype(jnp.bfloat16)
        ft_parts.append(lax.dot_general(wf_ref[...], h, nt, preferred_element_type=jnp.float32))
        krow = jnp.dot(h, wrow_ref[...], preferred_element_type=jnp.float32)
        kd_ref[0, rows] = krow[:, :DIFF_WIDTH].astype(jnp.bfloat16)
        for u in range(FOX_UNITS):
            kf_ref[0, rows, 256 * u:256 * u + 128] = (
                krow[:, DIFF_WIDTH + 128 * u:DIFF_WIDTH + 128 * (u + 1)].astype(jnp.bfloat16))
        for n in range(len(outs)):
            column_outputs(c, h, n)

    ft = jnp.concatenate(ft_parts, axis=1) + bf_ref[...]
    ls = jnp.minimum(ft, 0.0) - jnp.log1p(jnp.exp(-jnp.abs(ft)))
    lane = lax.broadcasted_iota(jnp.int32, ls.shape, 1)
    sh = 1
    while sh < tm:
        ls = ls + jnp.where(lane >= sh, pltpu.roll(ls, sh, axis=1), 0.0)
        sh *= 2

    @pl.when(pl.program_id(1) == 0)
    def _():
        carry_ref[...] = jnp.zeros_like(carry_ref)

    cum = ls + carry_ref[...]
    carry_ref[...] = cum[:, tm - 1:tm]
    cum = cum * LOG2E

    hi = cum.astype(jnp.bfloat16)
    r1 = cum - hi.astype(jnp.float32)
    mid = r1.astype(jnp.bfloat16)
    lo = (r1 - mid.astype(jnp.float32)).astype(jnp.bfloat16)
    parts = jnp.concatenate([hi, mid, lo, jnp.zeros((128 - 3 * XROWS, tm), jnp.bfloat16)], axis=0)
    xq = jnp.dot(pq_ref[...], parts, preferred_element_type=jnp.float32) + oq_ref[...]
    xq = xq.astype(jnp.bfloat16)
    for c in range(nsub):
        xq_ref[0, c] = xq[:, c * TQ:(c + 1) * TQ]
    xk = jnp.dot(pk_ref[...], parts, preferred_element_type=jnp.float32) + ok_ref[...]
    for u in range(FOX_UNITS):
        kf_ref[0, :, 256 * u + 128:256 * (u + 1)] = (
            xk[128 * u:128 * (u + 1), :].T.astype(jnp.bfloat16))


def _placement_constants():
    pq = np.zeros((FOX_UNITS * 2 * XROWS, 128), np.float32)
    oq = np.zeros((FOX_UNITS * 2 * XROWS, 1), np.float32)
    pk = np.zeros((FOX_UNITS * 128, 128), np.float32)
    ok = np.zeros((FOX_UNITS * 128, 1), np.float32)
    for u in range(FOX_UNITS):
        for a in range(2):
            head = 2 * u + a
            for r in range(3):
                pq[2 * XROWS * u + XROWS * a + r, XROWS * r + head] = 1.0
                oq[2 * XROWS * u + XROWS * a + 3 + r, 0] = 1.0
                pk[128 * u + XROWS * a + 3 + r, XROWS * r + head] = -1.0
                ok[128 * u + XROWS * a + r, 0] = 1.0
    return (jnp.asarray(pq, jnp.bfloat16), jnp.asarray(pk, jnp.bfloat16),
            jnp.asarray(oq), jnp.asarray(ok))


def _proj(x, g, wrow, wcol, wf, bf_col):
    B, S, D = x.shape
    tm = TM_PROJ
    nsub = tm // TQ
    nq = S // TQ
    pq, pk, oq, ok = _placement_constants()
    const = lambda shape: pl.BlockSpec(shape, lambda b, t: (0,) * len(shape))
    colspec = lambda rows: pl.BlockSpec((1, nsub, rows, TQ), lambda b, t: (b, t, 0, 0))
    bf16 = jnp.bfloat16
    return pl.pallas_call(
        _proj_kernel,
        grid=(B, S // tm),
        in_specs=[
            pl.BlockSpec((1, tm, D), lambda b, t: (b, t, 0)),
            const((1, D)), const(wrow.shape), const(wcol.shape), const(wf.shape), const(bf_col.shape),
            const(pq.shape), const(pk.shape), const(oq.shape), const(ok.shape),
        ],
        out_specs=[
            pl.BlockSpec((1, tm, DIFF_WIDTH), lambda b, t: (b, t, 0)),
            pl.BlockSpec((1, tm, 2 * FOX_WIDTH), lambda b, t: (b, t, 0)),
            colspec(512), colspec(DIFF_HEADS * DV_ROWS), colspec(512), colspec(FOX_HEADS * FV_ROWS),
            colspec(FOX_UNITS * 2 * XROWS),
        ],
        out_shape=[
            jax.ShapeDtypeStruct((B, S, DIFF_WIDTH), bf16),
            jax.ShapeDtypeStruct((B, S, 2 * FOX_WIDTH), bf16),
            jax.ShapeDtypeStruct((B, nq, 512, TQ), bf16),
            jax.ShapeDtypeStruct((B, nq, DIFF_HEADS * DV_ROWS, TQ), bf16),
            jax.ShapeDtypeStruct((B, nq, 512, TQ), bf16),
            jax.ShapeDtypeStruct((B, nq, FOX_HEADS * FV_ROWS, TQ), bf16),
            jax.ShapeDtypeStruct((B, nq, FOX_UNITS * 2 * XROWS, TQ), bf16),
        ],
        scratch_shapes=[pltpu.VMEM((XROWS, 1), jnp.float32)],
        compiler_params=pltpu.CompilerParams(
            dimension_semantics=("arbitrary", "arbitrary"), vmem_limit_bytes=VMEM_LIMIT),
        name="proj",
    )(x, g, wrow, wcol, wf, bf_col, pq, pk, oq, ok)


def _t5_buckets(n):
    max_exact = N_BUCKETS // 2
    dist = np.arange(n)
    d = np.maximum(dist, 1).astype(np.float32)
    large = np.float32(max_exact) + (np.log(d / np.float32(max_exact))
                                     / np.float32(math.log(MAX_DISTANCE / max_exact))
                                     * np.float32(N_BUCKETS - max_exact))
    large = np.minimum(large.astype(np.int32), N_BUCKETS - 1)
    return np.where(dist < max_exact, dist, large).astype(np.int32)


def _bias_kernel(rel_ref, idx_ref, out_ref):
    hd = pl.program_id(0)
    idx = idx_ref[...]
    val = jnp.full(idx.shape, NEG, jnp.float32)
    for bkt in range(N_BUCKETS):
        val = jnp.where(idx == bkt, rel_ref[bkt, hd] * LOG2E, val)
    out_ref[0] = val


def _bias_tiles(rel_bias, S):
    buckets = _t5_buckets(S)
    far = int(buckets[-1])
    assert np.all(buckets[TQ + 1:] == far)
    s_loc = np.arange(TK)[:, None]
    t_loc = np.arange(TQ)[None, :]
    rel0 = t_loc - s_loc
    idx0 = np.where(rel0 >= 0, buckets[np.maximum(rel0, 0)], -1)
    idx1 = buckets[TQ + t_loc - s_loc]
    idx = jnp.asarray(np.concatenate([idx1, idx0], axis=0).astype(np.int32))
    tiles = pl.pallas_call(
        _bias_kernel,
        grid=(DIFF_HEADS,),
        in_specs=[pl.BlockSpec(memory_space=pltpu.SMEM),
                  pl.BlockSpec((2 * TK, TQ), lambda h: (0, 0))],
        out_specs=pl.BlockSpec((1, 2 * TK, TQ), lambda h: (h, 0, 0)),
        out_shape=jax.ShapeDtypeStruct((DIFF_HEADS, 2 * TK, TQ), jnp.float32),
        name="t5_bias_tiles",
    )(rel_bias, idx)
    return tiles, far


def _flash_update(m_ref, acc_ref, a, st, vt, shift=None):
    m_old = m_ref[a]
    mx = jnp.max(st, axis=0, keepdims=True)
    if shift is None:
        m_new = jnp.maximum(m_old, mx)
        x = st - m_new
    else:
        m_new = jnp.maximum(m_old, mx + shift)
        x = st - (m_new - shift)
    p = jnp.exp2(x.astype(jnp.bfloat16))
    alpha = jnp.exp2(m_old - m_new)
    acc_ref[a] = alpha * acc_ref[a] + jnp.dot(vt, p, preferred_element_type=jnp.float32)
    m_ref[a] = m_new


def _flash_init(m_ref, acc_ref):
    m_ref[...] = jnp.full(m_ref.shape, NEG, jnp.float32)
    acc_ref[...] = jnp.zeros_like(acc_ref)


def _run_chains(n, st_ref, score_fn, update_fn, j, rows, prefetch_j):
    pending = [(c, st_ref[c, :rows]) for c in range(LOOKAHEAD)]
    for c in range(LOOKAHEAD, n):
        pending.append((c, score_fn(c, j, rows)))
        update_fn(*pending.pop(0))
    for c in range(LOOKAHEAD):
        if prefetch_j is not None:
            st_ref[c] = score_fn(c, prefetch_j, 2 * TK)
        update_fn(*pending.pop(0))


def _value_rows(v_ref, j, rows, r0, nrows):
    blocks = [v_ref[0, j + b, r0:r0 + nrows] for b in range(rows // TK)]
    return blocks[0] if len(blocks) == 1 else jnp.concatenate(blocks, axis=1)


def _stage_keys(k_ref, kbuf_ref, first):
    @pl.when(first)
    def _():
        def body(t, carry):
            r = pl.multiple_of(t * TK, TK)
            kbuf_ref[pl.ds(r, TK), :] = k_ref[0, pl.ds(r, TK), :]
            return carry

        lax.fori_loop(0, k_ref.shape[1] // TK, body, 0)


def _prime_chains(st_ref, score_fn):
    for c in range(LOOKAHEAD):
        st_ref[c] = score_fn(c, 0, 2 * TK)


def _key_block_schedule(i, step):
    nfar = jnp.maximum(i - 1, 0)

    def far_pair(jj, carry):
        step(2 * jj, 2 * TK, "far", 2 * jj + 2)
        return carry

    lax.fori_loop(0, lax.shift_right_logical(nfar, 1), far_pair, 0)

    @pl.when((nfar & 1) == 1)
    def _():
        step(i - 2, TK, "far", i - 1)

    @pl.when(i >= 1)
    def _():
        step(i - 1, 2 * TK, "last", None)

    @pl.when(i == 0)
    def _():
        step(0, TK, "diag_only", None)


def _diff_kernel(far_bucket, lambda_init,
                 rel_ref, k_ref, q_ref, v_ref, bias_ref, lq1_ref, lk1_ref, lq2_ref, lk2_ref, g_ref,
                 o_ref, qpad_ref, m_ref, acc_ref, st_ref, kbuf_ref):
    i = pl.program_id(1)
    d = DIFF_HEAD_DIM
    _stage_keys(k_ref, kbuf_ref, i == 0)
    qpad_ref[...] = jnp.zeros_like(qpad_ref)
    for c in range(2 * DIFF_HEADS):
        r0 = d * (c % 4)
        qpad_ref[c, r0:r0 + d] = q_ref[0, 0, d * c:d * (c + 1)]
    _flash_init(m_ref, acc_ref)

    def score(c, j, rows):
        pair = c // 4
        kblk = kbuf_ref[pl.ds(pl.multiple_of(j * TK, TK), rows), 256 * pair:256 * (pair + 1)]
        return jnp.dot(kblk, qpad_ref[c], preferred_element_type=jnp.float32)

    def step(j, rows, kind, prefetch_j):
        def update(c, st):
            hd = c // 2
            vt = _value_rows(v_ref, j, rows, DV_ROWS * hd, DV_ROWS)
            if kind == "far":
                _flash_update(m_ref, acc_ref, c, st, vt, shift=rel_ref[far_bucket, hd] * LOG2E)
            else:
                bias = bias_ref[hd] if kind == "last" else bias_ref[hd, TK:]
                _flash_update(m_ref, acc_ref, c, bias + st, vt)

        _run_chains(2 * DIFF_HEADS, st_ref, score, update, j, rows, prefetch_j)

    _prime_chains(st_ref, score)
    _key_block_schedule(i, step)

    lam = (jnp.exp(jnp.sum(lq1_ref[...] * lk1_ref[...], axis=-1, keepdims=True))
           - jnp.exp(jnp.sum(lq2_ref[...] * lk2_ref[...], axis=-1, keepdims=True))
           + lambda_init)
    for hd in range(DIFF_HEADS):
        dv = 2 * d
        o0 = acc_ref[2 * hd, :dv] * (1.0 / acc_ref[2 * hd, dv:dv + 1])
        o1 = acc_ref[2 * hd + 1, :dv] * (1.0 / acc_ref[2 * hd + 1, dv:dv + 1])
        od = o0 - lam * o1
        ms = jnp.mean(od * od, axis=0, keepdims=True)
        y = (od * lax.rsqrt(ms + EPS)).T * g_ref[...]
        o_ref[0, :, 128 * hd:128 * (hd + 1)] = (y * (1.0 - lambda_init)).astype(o_ref.dtype)


def _diff_attention(kd, qd, vd, bias_tiles, far_bucket, rel_bias, lq1, lk1, lq2, lk2, g, lambda_init):
    B, S, _ = kd.shape
    nq = S // TQ
    nmaps = 2 * DIFF_HEADS
    vec = lambda n: pl.BlockSpec((1, n), lambda b, i: (0, 0))
    return pl.pallas_call(
        functools.partial(_diff_kernel, far_bucket, lambda_init),
        grid=(B, nq),
        in_specs=[
            pl.BlockSpec(memory_space=pltpu.SMEM),
            pl.BlockSpec((1, S, DIFF_WIDTH), lambda b, i: (b, 0, 0)),
            pl.BlockSpec((1, 1, DIFF_WIDTH, TQ), lambda b, i: (b, i, 0, 0)),
            pl.BlockSpec((1, nq, DIFF_HEADS * DV_ROWS, TK), lambda b, i: (b, 0, 0, 0)),
            pl.BlockSpec((DIFF_HEADS, 2 * TK, TQ), lambda b, i: (0, 0, 0)),
            vec(DIFF_HEAD_DIM), vec(DIFF_HEAD_DIM), vec(DIFF_HEAD_DIM), vec(DIFF_HEAD_DIM),
            vec(2 * DIFF_HEAD_DIM),
        ],
        out_specs=pl.BlockSpec((1, TQ, DIFF_WIDTH), lambda b, i: (b, i, 0)),
        out_shape=jax.ShapeDtypeStruct((B, S, DIFF_WIDTH), jnp.bfloat16),
        scratch_shapes=[
            pltpu.VMEM((nmaps, 256, TQ), jnp.bfloat16),
            pltpu.VMEM((nmaps, 1, TQ), jnp.float32),
            pltpu.VMEM((nmaps, DV_ROWS, TQ), jnp.float32),
            pltpu.VMEM((LOOKAHEAD, 2 * TK, TQ), jnp.float32),
            pltpu.VMEM((S, DIFF_WIDTH), jnp.bfloat16),
        ],
        compiler_params=pltpu.CompilerParams(
            dimension_semantics=("parallel", "arbitrary"), vmem_limit_bytes=VMEM_LIMIT),
        name="diff_attention",
    )(rel_bias, kd, qd, vd, bias_tiles, lq1, lk1, lq2, lk2, g)


def _fox_kernel(k_ref, q_ref, xq_ref, v_ref, o_ref, qpad_ref, m_ref, acc_ref, st_ref, kbuf_ref):
    i = pl.program_id(1)
    d = FOX_HEAD_DIM
    _stage_keys(k_ref, kbuf_ref, i == 0)
    qpad_ref[...] = jnp.zeros_like(qpad_ref)
    for hd in range(FOX_HEADS):
        a = hd % 2
        qpad_ref[hd, d * a:d * (a + 1)] = q_ref[0, 0, d * hd:d * (hd + 1)]
        qpad_ref[hd, 128 + XROWS * a:128 + XROWS * (a + 1)] = xq_ref[0, 0, XROWS * hd:XROWS * (hd + 1)]
    _flash_init(m_ref, acc_ref)

    def score(hd, j, rows):
        u = hd // 2
        kblk = kbuf_ref[pl.ds(pl.multiple_of(j * TK, TK), rows), 256 * u:256 * (u + 1)]
        return jnp.dot(kblk, qpad_ref[hd], preferred_element_type=jnp.float32)

    def step(j, rows, kind, prefetch_j):
        def update(hd, st):
            if kind != "far":
                row = lax.broadcasted_iota(jnp.int32, st.shape, 0) - (rows - TK)
                col = lax.broadcasted_iota(jnp.int32, st.shape, 1)
                st = jnp.where(row > col, NEG, st)
            _flash_update(m_ref, acc_ref, hd, st, _value_rows(v_ref, j, rows, FV_ROWS * hd, FV_ROWS))

        _run_chains(FOX_HEADS, st_ref, score, update, j, rows, prefetch_j)

    _prime_chains(st_ref, score)
    _key_block_schedule(i, step)

    for u in range(FOX_UNITS):
        o = jnp.concatenate([acc_ref[hd, :d] * (1.0 / acc_ref[hd, d:d + 1]) for hd in (2 * u, 2 * u + 1)],
                            axis=0)
        o_ref[0, :, 128 * u:128 * (u + 1)] = o.T.astype(o_ref.dtype)


def _fox_attention(kf, qf, xq, vf):
    B, S, _ = kf.shape
    nq = S // TQ
    return pl.pallas_call(
        _fox_kernel,
        grid=(B, nq),
        in_specs=[
            pl.BlockSpec((1, S, 2 * FOX_WIDTH), lambda b, i: (b, 0, 0)),
            pl.BlockSpec((1, 1, FOX_WIDTH, TQ), lambda b, i: (b, i, 0, 0)),
            pl.BlockSpec((1, 1, FOX_HEADS * XROWS, TQ), lambda b, i: (b, i, 0, 0)),
            pl.BlockSpec((1, nq, FOX_HEADS * FV_ROWS, TK), lambda b, i: (b, 0, 0, 0)),
        ],
        out_specs=pl.BlockSpec((1, TQ, FOX_WIDTH), lambda b, i: (b, i, 0)),
        out_shape=jax.ShapeDtypeStruct((B, S, FOX_WIDTH), jnp.bfloat16),
        scratch_shapes=[
            pltpu.VMEM((FOX_HEADS, 256, TQ), jnp.bfloat16),
            pltpu.VMEM((FOX_HEADS, 1, TQ), jnp.float32),
            pltpu.VMEM((FOX_HEADS, FV_ROWS, TQ), jnp.float32),
            pltpu.VMEM((LOOKAHEAD, 2 * TK, TQ), jnp.float32),
            pltpu.VMEM((S, 2 * FOX_WIDTH), jnp.bfloat16),
        ],
        compiler_params=pltpu.CompilerParams(
            dimension_semantics=("parallel", "arbitrary"), vmem_limit_bytes=VMEM_LIMIT),
        name="fox_attention",
    )(kf, qf, xq, vf)


def _post_kernel(md_ref, mf_ref, x_ref, wod_ref, wof_ref, wup_ref, wdn_ref, g1_ref, g2_ref, g3_ref,
                 o_ref):
    a = (jnp.dot(md_ref[...], wod_ref[...], preferred_element_type=jnp.float32)
         + jnp.dot(mf_ref[...], wof_ref[...], preferred_element_type=jnp.float32))
    x1 = x_ref[...] + _rms(a, g1_ref[...])
    h2 = _rms(x1, g2_ref[...]).astype(jnp.bfloat16)
    acc = None
    for c in range(D_FF // FF_CHUNK):
        u = jnp.dot(h2, wup_ref[:, c * FF_CHUNK:(c + 1) * FF_CHUNK], preferred_element_type=jnp.float32)
        u = jnp.square(jnp.maximum(u, 0.0)).astype(jnp.bfloat16)
        part = jnp.dot(u, wdn_ref[c * FF_CHUNK:(c + 1) * FF_CHUNK, :], preferred_element_type=jnp.float32)
        acc = part if acc is None else acc + part
    o_ref[...] = x1 + _rms(acc, g3_ref[...])


def _post(mix_d, mix_f, x2d, wod, wof, wup, wdn, g1, g2, g3):
    R, D = x2d.shape
    tm = TM_POST
    resident = lambda shape: pl.BlockSpec(shape, lambda r: (0,) * len(shape),
                                          pipeline_mode=pl.Buffered(1))
    return pl.pallas_call(
        _post_kernel,
        grid=(R // tm,),
        in_specs=[
            pl.BlockSpec((tm, DIFF_WIDTH), lambda r: (r, 0)),
            pl.BlockSpec((tm, FOX_WIDTH), lambda r: (r, 0)),
            pl.BlockSpec((tm, D), lambda r: (r, 0)),
            resident(wod.shape), resident(wof.shape), resident(wup.shape), resident(wdn.shape),
            resident((1, D)), resident((1, D)), resident((1, D)),
        ],
        out_specs=pl.BlockSpec((tm, D), lambda r: (r, 0)),
        out_shape=jax.ShapeDtypeStruct((R, D), jnp.float32),
        compiler_params=pltpu.CompilerParams(
            dimension_semantics=("parallel",), vmem_limit_bytes=VMEM_LIMIT),
        name="post",
    )(mix_d, mix_f, x2d, wod, wof, wup, wdn, g1, g2, g3)


def kernel(x, ln_attn_pre, w_in, b_f, lam_q1, lam_k1, lam_q2, lam_k2, subln_g, rel_bias,
           w_out, ln_attn_post, ln_mlp_pre, w_up, w_down, ln_mlp_post):
    B, S, D = x.shape
    depth = w_in.shape[0]
    bf16 = jnp.bfloat16
    bias_tiles, far_bucket = _bias_tiles(rel_bias, S)
    dw, fw = DIFF_WIDTH, FOX_WIDTH
    for l in range(depth):
        lambda_init = 0.8 - 0.6 * math.exp(-0.3 * l)
        w = w_in[l]
        o_dq, o_dk, o_dv, o_fq, o_fk, o_fv, o_g = 0, dw, 2 * dw, 3 * dw, 3 * dw + fw, 3 * dw + 2 * fw, 3 * dw + 3 * fw
        wrow = jnp.concatenate([w[:, o_dk:o_dk + dw], w[:, o_fk:o_fk + fw]], axis=1).astype(bf16)
        wcol = jnp.concatenate([w[:, o_dq:o_dq + dw], w[:, o_dv:o_dv + dw],
                                w[:, o_fq:o_fq + fw], w[:, o_fv:o_fv + fw]], axis=1).T.astype(bf16)
        wf = jnp.zeros((XROWS, D), bf16).at[:FOX_HEADS].set(w[:, o_g:o_g + FOX_HEADS].T.astype(bf16))
        bf_col = jnp.zeros((XROWS, 1), jnp.float32).at[:FOX_HEADS, 0].set(b_f[l].astype(jnp.float32))

        kd, kf, qd, vd, qf, vf, xq = _proj(x, ln_attn_pre[l][None, :], wrow, wcol, wf, bf_col)
        mix_d = _diff_attention(kd, qd, vd, bias_tiles, far_bucket, rel_bias,
                                lam_q1[l][None, :], lam_k1[l][None, :], lam_q2[l][None, :],
                                lam_k2[l][None, :], subln_g[l][None, :], lambda_init)
        mix_f = _fox_attention(kf, qf, xq, vf)
        wo = w_out[l].astype(bf16)
        y = _post(mix_d.reshape(B * S, dw), mix_f.reshape(B * S, fw), x.reshape(B * S, D),
                  wo[:dw], wo[dw:], w_up[l].astype(bf16), w_down[l].astype(bf16),
                  ln_attn_post[l][None, :], ln_mlp_pre[l][None, :], ln_mlp_post[l][None, :])
        x = y.reshape(B, S, D)
    return x
```

```python
import functools
import math

import numpy as np
import jax
import jax.numpy as jnp
from jax import lax
from jax.experimental import pallas as pl
from jax.experimental.pallas import tpu as pltpu

D_MODEL = 1024
DIFF_HEADS = 4
DIFF_HEAD_DIM = 64
DIFF_WIDTH = DIFF_HEADS * 2 * DIFF_HEAD_DIM
FOX_HEADS = 8
FOX_HEAD_DIM = 64
FOX_WIDTH = FOX_HEADS * FOX_HEAD_DIM
FOX_UNITS = FOX_HEADS // 2
D_FF = 4 * D_MODEL
N_BUCKETS = 32
MAX_DISTANCE = 128
EPS = 1e-6
NEG = -1e30
QK_SCALE = DIFF_HEAD_DIM ** -0.5
LOG2E = math.log2(math.e)

TQ = 512
TK = 256
TM_PROJ = 512
TM_POST = 512
FF_CHUNK = 1024
LOOKAHEAD = 4
ONES_ROWS = 16
DV_ROWS = 2 * DIFF_HEAD_DIM + ONES_ROWS
FV_ROWS = FOX_HEAD_DIM + ONES_ROWS
XROWS = 16
VMEM_LIMIT = 56 * 1024 * 1024


def _rms(x, g):
    ms = jnp.mean(x * x, axis=-1, keepdims=True)
    return x * lax.rsqrt(ms + EPS) * g


def _proj_kernel(x_ref, g_ref, wrow_ref, wcol_ref, wf_ref, bf_ref, pq_ref, pk_ref, oq_ref, ok_ref,
                 kd_ref, kf_ref, qd_ref, vd_ref, qf_ref, vf_ref, xq_ref, carry_ref):
    tm = x_ref.shape[1]
    nsub = tm // TK
    nt = (((1,), (1,)), ((), ()))
    ones = jnp.ones((ONES_ROWS, TK), jnp.bfloat16)
    outs = ((qd_ref, None, None), (vd_ref, 2 * DIFF_HEAD_DIM, DV_ROWS),
            (qf_ref, None, None), (vf_ref, FOX_HEAD_DIM, FV_ROWS))

    def column_outputs(c, h, n):
        o_ref, hrows, orows = outs[n]
        colt = lax.dot_general(wcol_ref[n * 512:(n + 1) * 512, :], h, nt,
                               preferred_element_type=jnp.float32)
        if hrows is None:
            o_ref[0, 0, :, c * TK:(c + 1) * TK] = (colt * (QK_SCALE * LOG2E)).astype(jnp.bfloat16)
        else:
            colt = colt.astype(jnp.bfloat16)
            for hd in range(512 // hrows):
                o_ref[0, c, orows * hd:orows * hd + hrows] = colt[hrows * hd:hrows * (hd + 1)]
                o_ref[0, c, orows * hd + hrows:orows * (hd + 1)] = ones

    ft_parts = []
    for c in range(nsub):
        rows = slice(c * TK, (c + 1) * TK)
        h = _rms(x_ref[0, rows], g_ref[...]).astype(jnp.bfloat16)
        ft_parts.append(lax.dot_general(wf_ref[...], h, nt, preferred_element_type=jnp.float32))
        krow = jnp.dot(h, wrow_ref[...], preferred_element_type=jnp.float32)
        kd_ref[0, rows] = krow[:, :DIFF_WIDTH].astype(jnp.bfloat16)
        for u in range(FOX_UNITS):
            kf_ref[0, rows, 256 * u:256 * u + 128] = (
                krow[:, DIFF_WIDTH + 128 * u:DIFF_WIDTH + 128 * (u + 1)].astype(jnp.bfloat16))
        for n in range(len(outs)):
            column_outputs(c, h, n)

    ft = jnp.concatenate(ft_parts, axis=1) + bf_ref[...]
    ls = jnp.minimum(ft, 0.0) - jnp.log1p(jnp.exp(-jnp.abs(ft)))
    lane = lax.broadcasted_iota(jnp.int32, ls.shape, 1)
    sh = 1
    while sh < tm:
        ls = ls + jnp.where(lane >= sh, pltpu.roll(ls, sh, axis=1), 0.0)
        sh *= 2

    @pl.when(pl.program_id(1) == 0)
    def _():
        carry_ref[...] = jnp.zeros_like(carry_ref)

    cum = ls + carry_ref[...]
    carry_ref[...] = cum[:, tm - 1:tm]
    cum = cum * LOG2E

    hi = cum.astype(jnp.bfloat16)
    r1 = cum - hi.astype(jnp.float32)
    mid = r1.astype(jnp.bfloat16)
    lo = (r1 - mid.astype(jnp.float32)).astype(jnp.bfloat16)
    parts = jnp.concatenate([hi, mid, lo, jnp.zeros((128 - 3 * XROWS, tm), jnp.bfloat16)], axis=0)
    xq = jnp.dot(pq_ref[...], parts, preferred_element_type=jnp.float32) + oq_ref[...]
    xq_ref[0, 0] = xq.astype(jnp.bfloat16)
    xk = jnp.dot(pk_ref[...], parts, preferred_element_type=jnp.float32) + ok_ref[...]
    for u in range(FOX_UNITS):
        kf_ref[0, :, 256 * u + 128:256 * (u + 1)] = (
            xk[128 * u:128 * (u + 1), :].T.astype(jnp.bfloat16))


def _placement_constants():
    pq = np.zeros((FOX_UNITS * 2 * XROWS, 128), np.float32)
    oq = np.zeros((FOX_UNITS * 2 * XROWS, 1), np.float32)
    pk = np.zeros((FOX_UNITS * 128, 128), np.float32)
    ok = np.zeros((FOX_UNITS * 128, 1), np.float32)
    for u in range(FOX_UNITS):
        for a in range(2):
            head = 2 * u + a
            for r in range(3):
                pq[2 * XROWS * u + XROWS * a + r, XROWS * r + head] = 1.0
                oq[2 * XROWS * u + XROWS * a + 3 + r, 0] = 1.0
                pk[128 * u + XROWS * a + 3 + r, XROWS * r + head] = -1.0
                ok[128 * u + XROWS * a + r, 0] = 1.0
    return (jnp.asarray(pq, jnp.bfloat16), jnp.asarray(pk, jnp.bfloat16),
            jnp.asarray(oq), jnp.asarray(ok))


def _proj(x, g, wrow, wcol, wf, bf_col):
    B, S, D = x.shape
    tm = TM_PROJ
    assert tm == TQ
    nq, nk = S // TQ, S // TK
    pq, pk, oq, ok = _placement_constants()
    const = lambda shape: pl.BlockSpec(shape, lambda b, t: (0,) * len(shape))
    qspec = lambda rows: pl.BlockSpec((1, 1, rows, TQ), lambda b, t: (b, t, 0, 0))
    vspec = lambda rows: pl.BlockSpec((1, tm // TK, rows, TK), lambda b, t: (b, t, 0, 0))
    bf16 = jnp.bfloat16
    return pl.pallas_call(
        _proj_kernel,
        grid=(B, S // tm),
        in_specs=[
            pl.BlockSpec((1, tm, D), lambda b, t: (b, t, 0)),
            const((1, D)), const(wrow.shape), const(wcol.shape), const(wf.shape), const(bf_col.shape),
            const(pq.shape), const(pk.shape), const(oq.shape), const(ok.shape),
        ],
        out_specs=[
            pl.BlockSpec((1, tm, DIFF_WIDTH), lambda b, t: (b, t, 0)),
            pl.BlockSpec((1, tm, 2 * FOX_WIDTH), lambda b, t: (b, t, 0)),
            qspec(512), vspec(DIFF_HEADS * DV_ROWS), qspec(512), vspec(FOX_HEADS * FV_ROWS),
            qspec(FOX_UNITS * 2 * XROWS),
        ],
        out_shape=[
            jax.ShapeDtypeStruct((B, S, DIFF_WIDTH), bf16),
            jax.ShapeDtypeStruct((B, S, 2 * FOX_WIDTH), bf16),
            jax.ShapeDtypeStruct((B, nq, 512, TQ), bf16),
            jax.ShapeDtypeStruct((B, nk, DIFF_HEADS * DV_ROWS, TK), bf16),
            jax.ShapeDtypeStruct((B, nq, 512, TQ), bf16),
            jax.ShapeDtypeStruct((B, nk, FOX_HEADS * FV_ROWS, TK), bf16),
            jax.ShapeDtypeStruct((B, nq, FOX_UNITS * 2 * XROWS, TQ), bf16),
        ],
        scratch_shapes=[pltpu.VMEM((XROWS, 1), jnp.float32)],
        compiler_params=pltpu.CompilerParams(
            dimension_semantics=("arbitrary", "arbitrary"), vmem_limit_bytes=VMEM_LIMIT),
        name="proj",
    )(x, g, wrow, wcol, wf, bf_col, pq, pk, oq, ok)


def _t5_buckets(n):
    max_exact = N_BUCKETS // 2
    dist = np.arange(n)
    d = np.maximum(dist, 1).astype(np.float32)
    large = np.float32(max_exact) + (np.log(d / np.float32(max_exact))
                                     / np.float32(math.log(MAX_DISTANCE / max_exact))
                                     * np.float32(N_BUCKETS - max_exact))
    large = np.minimum(large.astype(np.int32), N_BUCKETS - 1)
    return np.where(dist < max_exact, dist, large).astype(np.int32)


def _bias_kernel(far_bucket, rel_ref, idx_ref, out_ref):
    hd = pl.program_id(0)
    for t in range(2):
        idx = idx_ref[t]
        val = jnp.full(idx.shape, NEG, jnp.float32)
        for bkt in range(N_BUCKETS):
            val = jnp.where(idx == bkt, rel_ref[bkt, hd] * LOG2E, val)
        out_ref[0, t] = val
        if t == 1:
            out_ref[0, 2] = val - rel_ref[far_bucket, hd] * LOG2E


def _bias_tiles(rel_bias, S):
    buckets = _t5_buckets(S)
    far = int(buckets[-1])
    assert np.all(buckets[TK + 1:] == far)
    s_loc = np.arange(TK)[:, None]
    t_loc = np.arange(TK)[None, :]
    rel0 = t_loc - s_loc
    idx0 = np.where(rel0 >= 0, buckets[np.maximum(rel0, 0)], -1)
    idx1 = buckets[TK + t_loc - s_loc]
    idx = jnp.asarray(np.stack([idx0, idx1]).astype(np.int32))
    tiles = pl.pallas_call(
        functools.partial(_bias_kernel, far),
        grid=(DIFF_HEADS,),
        in_specs=[pl.BlockSpec(memory_space=pltpu.SMEM),
                  pl.BlockSpec((2, TK, TK), lambda h: (0, 0, 0))],
        out_specs=pl.BlockSpec((1, 3, TK, TK), lambda h: (h, 0, 0, 0)),
        out_shape=jax.ShapeDtypeStruct((DIFF_HEADS, 3, TK, TK), jnp.float32),
        name="t5_bias_tiles",
    )(rel_bias, idx)
    return tiles, far


ALL = slice(None)
UPPER = slice(TK, 2 * TK)


def _flash_update(m_ref, acc_ref, a, st, vt, shift=None, lanes=ALL):
    m_old = m_ref[a, :, lanes]
    mx = jnp.max(st, axis=0, keepdims=True)
    if shift is None:
        m_new = jnp.maximum(m_old, mx)
        x = st - m_new
    else:
        m_new = jnp.maximum(m_old, mx + shift)
        x = st - (m_new - shift)
    p = jnp.exp2(x.astype(jnp.bfloat16))
    alpha = jnp.exp2(m_old - m_new)
    acc_ref[a, :, lanes] = alpha * acc_ref[a, :, lanes] + jnp.dot(vt, p, preferred_element_type=jnp.float32)
    m_ref[a, :, lanes] = m_new


def _flash_init(m_ref, acc_ref):
    m_ref[...] = jnp.full(m_ref.shape, NEG, jnp.float32)
    acc_ref[...] = jnp.zeros_like(acc_ref)


def _run_chains(n, st_ref, score_fn, update_fn, j, rows, prefetch_j, row0=0, lanes=ALL):
    pending = [(c, st_ref[c, row0:row0 + rows, lanes]) for c in range(LOOKAHEAD)]
    for c in range(LOOKAHEAD, n):
        pending.append((c, score_fn(c, j, rows, lanes)))
        update_fn(*pending.pop(0))
    for c in range(LOOKAHEAD):
        if prefetch_j is not None:
            st_ref[c] = score_fn(c, prefetch_j, 2 * TK, ALL)
        update_fn(*pending.pop(0))


def _prime_chains(st_ref, score_fn):
    for c in range(LOOKAHEAD):
        st_ref[c] = score_fn(c, 0, 2 * TK, ALL)


def _value_rows(v_ref, j, rows, r0, nrows):
    blocks = [v_ref[0, j + b, r0:r0 + nrows] for b in range(rows // TK)]
    return blocks[0] if len(blocks) == 1 else jnp.concatenate(blocks, axis=1)


def _stage_keys(k_ref, kbuf_ref, first):
    @pl.when(first)
    def _():
        def body(t, carry):
            r = pl.multiple_of(t * TK, TK)
            kbuf_ref[pl.ds(r, TK), :] = k_ref[0, pl.ds(r, TK), :]
            return carry

        lax.fori_loop(0, k_ref.shape[1] // TK, body, 0)


def _key_block_schedule(i, step):
    def far_pair(jj, carry):
        step(2 * jj, 2 * TK, "far", 2 * jj + 2, 0, ALL)
        return carry

    lax.fori_loop(0, jnp.maximum(i - 1, 0), far_pair, 0)

    @pl.when(i >= 1)
    def _():
        step(2 * i - 2, 2 * TK, "near", 2 * i, 0, ALL)

    step(2 * i, TK, "diag_a", None, 0, ALL)
    step(2 * i + 1, TK, "diag_b", None, TK, UPPER)


def _diff_kernel(far_bucket, lambda_init,
                 rel_ref, k_ref, q_ref, v_ref, bias_ref, lq1_ref, lk1_ref, lq2_ref, lk2_ref, g_ref,
                 o_ref, qpad_ref, m_ref, acc_ref, st_ref, kbuf_ref):
    i = pl.program_id(1)
    d = DIFF_HEAD_DIM
    _stage_keys(k_ref, kbuf_ref, i == 0)
    qpad_ref[...] = jnp.zeros_like(qpad_ref)
    for c in range(2 * DIFF_HEADS):
        r0 = d * (c % 4)
        qpad_ref[c, r0:r0 + d] = q_ref[0, 0, d * c:d * (c + 1)]
    _flash_init(m_ref, acc_ref)

    def score(c, j, rows, lanes):
        pair = c // 4
        kblk = kbuf_ref[pl.ds(pl.multiple_of(j * TK, TK), rows), 256 * pair:256 * (pair + 1)]
        return jnp.dot(kblk, qpad_ref[c, :, lanes], preferred_element_type=jnp.float32)

    def step(j, rows, kind, prefetch_j, row0, lanes):
        def update(c, st):
            hd = c // 2
            vt = _value_rows(v_ref, j, rows, DV_ROWS * hd, DV_ROWS)
            far_bias = rel_ref[far_bucket, hd] * LOG2E
            if kind == "far":
                _flash_update(m_ref, acc_ref, c, st, vt, shift=far_bias)
            elif kind == "near":
                lower = jnp.concatenate([bias_ref[hd, 2] + st[TK:, :TK], st[TK:, TK:]], axis=1)
                _flash_update(m_ref, acc_ref, c, jnp.concatenate([st[:TK], lower], axis=0), vt,
                              shift=far_bias)
            elif kind == "diag_a":
                bias = jnp.concatenate([bias_ref[hd, 0], bias_ref[hd, 1]], axis=1)
                _flash_update(m_ref, acc_ref, c, bias + st, vt)
            else:
                _flash_update(m_ref, acc_ref, c, bias_ref[hd, 0] + st, vt, lanes=lanes)

        _run_chains(2 * DIFF_HEADS, st_ref, score, update, j, rows, prefetch_j, row0, lanes)

    _prime_chains(st_ref, score)
    _key_block_schedule(i, step)

    lam = (jnp.exp(jnp.sum(lq1_ref[...] * lk1_ref[...], axis=-1, keepdims=True))
           - jnp.exp(jnp.sum(lq2_ref[...] * lk2_ref[...], axis=-1, keepdims=True))
           + lambda_init)
    for hd in range(DIFF_HEADS):
        dv = 2 * d
        o0 = acc_ref[2 * hd, :dv] * (1.0 / acc_ref[2 * hd, dv:dv + 1])
        o1 = acc_ref[2 * hd + 1, :dv] * (1.0 / acc_ref[2 * hd + 1, dv:dv + 1])
        od = o0 - lam * o1
        ms = jnp.mean(od * od, axis=0, keepdims=True)
        y = (od * lax.rsqrt(ms + EPS)).T * g_ref[...]
        o_ref[0, :, 128 * hd:128 * (hd + 1)] = (y * (1.0 - lambda_init)).astype(o_ref.dtype)


def _diff_attention(kd, qd, vd, bias_tiles, far_bucket, rel_bias, lq1, lk1, lq2, lk2, g, lambda_init):
    B, S, _ = kd.shape
    nq, nk = S // TQ, S // TK
    nmaps = 2 * DIFF_HEADS
    vec = lambda n: pl.BlockSpec((1, n), lambda b, i: (0, 0))
    return pl.pallas_call(
        functools.partial(_diff_kernel, far_bucket, lambda_init),
        grid=(B, nq),
        in_specs=[
            pl.BlockSpec(memory_space=pltpu.SMEM),
            pl.BlockSpec((1, S, DIFF_WIDTH), lambda b, i: (b, 0, 0)),
            pl.BlockSpec((1, 1, DIFF_WIDTH, TQ), lambda b, i: (b, i, 0, 0)),
            pl.BlockSpec((1, nk, DIFF_HEADS * DV_ROWS, TK), lambda b, i: (b, 0, 0, 0)),
            pl.BlockSpec((DIFF_HEADS, 3, TK, TK), lambda b, i: (0, 0, 0, 0)),
            vec(DIFF_HEAD_DIM), vec(DIFF_HEAD_DIM), vec(DIFF_HEAD_DIM), vec(DIFF_HEAD_DIM),
            vec(2 * DIFF_HEAD_DIM),
        ],
        out_specs=pl.BlockSpec((1, TQ, DIFF_WIDTH), lambda b, i: (b, i, 0)),
        out_shape=jax.ShapeDtypeStruct((B, S, DIFF_WIDTH), jnp.bfloat16),
        scratch_shapes=[
            pltpu.VMEM((nmaps, 256, TQ), jnp.bfloat16),
            pltpu.VMEM((nmaps, 1, TQ), jnp.float32),
            pltpu.VMEM((nmaps, DV_ROWS, TQ), jnp.float32),
            pltpu.VMEM((LOOKAHEAD, 2 * TK, TQ), jnp.float32),
            pltpu.VMEM((S, DIFF_WIDTH), jnp.bfloat16),
        ],
        compiler_params=pltpu.CompilerParams(
            dimension_semantics=("parallel", "arbitrary"), vmem_limit_bytes=VMEM_LIMIT),
        name="diff_attention",
    )(rel_bias, kd, qd, vd, bias_tiles, lq1, lk1, lq2, lk2, g)


def _fox_kernel(k_ref, q_ref, xq_ref, v_ref, o_ref, qpad_ref, m_ref, acc_ref, st_ref, kbuf_ref):
    i = pl.program_id(1)
    d = FOX_HEAD_DIM
    _stage_keys(k_ref, kbuf_ref, i == 0)
    qpad_ref[...] = jnp.zeros_like(qpad_ref)
    for hd in range(FOX_HEADS):
        a = hd % 2
        qpad_ref[hd, d * a:d * (a + 1)] = q_ref[0, 0, d * hd:d * (hd + 1)]
        qpad_ref[hd, 128 + XROWS * a:128 + XROWS * (a + 1)] = xq_ref[0, 0, XROWS * hd:XROWS * (hd + 1)]
    _flash_init(m_ref, acc_ref)

    def score(hd, j, rows, lanes):
        u = hd // 2
        kblk = kbuf_ref[pl.ds(pl.multiple_of(j * TK, TK), rows), 256 * u:256 * (u + 1)]
        return jnp.dot(kblk, qpad_ref[hd, :, lanes], preferred_element_type=jnp.float32)

    def step(j, rows, kind, prefetch_j, row0, lanes):
        def update(hd, st):
            if kind in ("diag_a", "diag_b"):
                row = lax.broadcasted_iota(jnp.int32, st.shape, 0)
                col = lax.broadcasted_iota(jnp.int32, st.shape, 1)
                st = jnp.where(row > col, NEG, st)
            _flash_update(m_ref, acc_ref, hd, st, _value_rows(v_ref, j, rows, FV_ROWS * hd, FV_ROWS),
                          lanes=lanes)

        _run_chains(FOX_HEADS, st_ref, score, update, j, rows, prefetch_j, row0, lanes)

    _prime_chains(st_ref, score)
    _key_block_schedule(i, step)

    for u in range(FOX_UNITS):
        o = jnp.concatenate([acc_ref[hd, :d] * (1.0 / acc_ref[hd, d:d + 1]) for hd in (2 * u, 2 * u + 1)],
                            axis=0)
        o_ref[0, :, 128 * u:128 * (u + 1)] = o.T.astype(o_ref.dtype)


def _fox_attention(kf, qf, xq, vf):
    B, S, _ = kf.shape
    nq, nk = S // TQ, S // TK
    return pl.pallas_call(
        _fox_kernel,
        grid=(B, nq),
        in_specs=[
            pl.BlockSpec((1, S, 2 * FOX_WIDTH), lambda b, i: (b, 0, 0)),
            pl.BlockSpec((1, 1, FOX_WIDTH, TQ), lambda b, i: (b, i, 0, 0)),
            pl.BlockSpec((1, 1, FOX_HEADS * XROWS, TQ), lambda b, i: (b, i, 0, 0)),
            pl.BlockSpec((1, nk, FOX_HEADS * FV_ROWS, TK), lambda b, i: (b, 0, 0, 0)),
        ],
        out_specs=pl.BlockSpec((1, TQ, FOX_WIDTH), lambda b, i: (b, i, 0)),
        out_shape=jax.ShapeDtypeStruct((B, S, FOX_WIDTH), jnp.bfloat16),
        scratch_shapes=[
            pltpu.VMEM((FOX_HEADS, 256, TQ), jnp.bfloat16),
            pltpu.VMEM((FOX_HEADS, 1, TQ), jnp.float32),
            pltpu.VMEM((FOX_HEADS, FV_ROWS, TQ), jnp.float32),
            pltpu.VMEM((LOOKAHEAD, 2 * TK, TQ), jnp.float32),
            pltpu.VMEM((S, 2 * FOX_WIDTH), jnp.bfloat16),
        ],
        compiler_params=pltpu.CompilerParams(
            dimension_semantics=("parallel", "arbitrary"), vmem_limit_bytes=VMEM_LIMIT),
        name="fox_attention",
    )(kf, qf, xq, vf)


def _post_kernel(md_ref, mf_ref, x_ref, wod_ref, wof_ref, wup_ref, wdn_ref, g1_ref, g2_ref, g3_ref,
                 o_ref):
    a = (jnp.dot(md_ref[...], wod_ref[...], preferred_element_type=jnp.float32)
         + jnp.dot(mf_ref[...], wof_ref[...], preferred_element_type=jnp.float32))
    x1 = x_ref[...] + _rms(a, g1_ref[...])
    h2 = _rms(x1, g2_ref[...]).astype(jnp.bfloat16)
    acc = None
    for c in range(D_FF // FF_CHUNK):
        u = jnp.dot(h2, wup_ref[:, c * FF_CHUNK:(c + 1) * FF_CHUNK], preferred_element_type=jnp.float32)
        u = jnp.square(jnp.maximum(u, 0.0)).astype(jnp.bfloat16)
        part = jnp.dot(u, wdn_ref[c * FF_CHUNK:(c + 1) * FF_CHUNK, :], preferred_element_type=jnp.float32)
        acc = part if acc is None else acc + part
    o_ref[...] = x1 + _rms(acc, g3_ref[...])


def _post(mix_d, mix_f, x2d, wod, wof, wup, wdn, g1, g2, g3):
    R, D = x2d.shape
    tm = TM_POST
    resident = lambda shape: pl.BlockSpec(shape, lambda r: (0,) * len(shape),
                                          pipeline_mode=pl.Buffered(1))
    return pl.pallas_call(
        _post_kernel,
        grid=(R // tm,),
        in_specs=[
            pl.BlockSpec((tm, DIFF_WIDTH), lambda r: (r, 0)),
            pl.BlockSpec((tm, FOX_WIDTH), lambda r: (r, 0)),
            pl.BlockSpec((tm, D), lambda r: (r, 0)),
            resident(wod.shape), resident(wof.shape), resident(wup.shape), resident(wdn.shape),
            resident((1, D)), resident((1, D)), resident((1, D)),
        ],
        out_specs=pl.BlockSpec((tm, D), lambda r: (r, 0)),
        out_shape=jax.ShapeDtypeStruct((R, D), jnp.float32),
        compiler_params=pltpu.CompilerParams(
            dimension_semantics=("parallel",), vmem_limit_bytes=VMEM_LIMIT),
        name="post",
    )(mix_d, mix_f, x2d, wod, wof, wup, wdn, g1, g2, g3)


def kernel(x, ln_attn_pre, w_in, b_f, lam_q1, lam_k1, lam_q2, lam_k2, subln_g, rel_bias,
           w_out, ln_attn_post, ln_mlp_pre, w_up, w_down, ln_mlp_post):
    B, S, D = x.shape
    depth = w_in.shape[0]
    bf16 = jnp.bfloat16
    bias_tiles, far_bucket = _bias_tiles(rel_bias, S)
    dw, fw = DIFF_WIDTH, FOX_WIDTH
    for l in range(depth):
        lambda_init = 0.8 - 0.6 * math.exp(-0.3 * l)
        w = w_in[l]
        o_dq, o_dk, o_dv, o_fq, o_fk, o_fv, o_g = 0, dw, 2 * dw, 3 * dw, 3 * dw + fw, 3 * dw + 2 * fw, 3 * dw + 3 * fw
        wrow = jnp.concatenate([w[:, o_dk:o_dk + dw], w[:, o_fk:o_fk + fw]], axis=1).astype(bf16)
        wcol = jnp.concatenate([w[:, o_dq:o_dq + dw], w[:, o_dv:o_dv + dw],
                                w[:, o_fq:o_fq + fw], w[:, o_fv:o_fv + fw]], axis=1).T.astype(bf16)
        wf = jnp.zeros((XROWS, D), bf16).at[:FOX_HEADS].set(w[:, o_g:o_g + FOX_HEADS].T.astype(bf16))
        bf_col = jnp.zeros((XROWS, 1), jnp.float32).at[:FOX_HEADS, 0].set(b_f[l].astype(jnp.float32))

        kd, kf, qd, vd, qf, vf, xq = _proj(x, ln_attn_pre[l][None, :], wrow, wcol, wf, bf_col)
        mix_d = _diff_attention(kd, qd, vd, bias_tiles, far_bucket, rel_bias,
                                lam_q1[l][None, :], lam_k1[l][None, :], lam_q2[l][None, :],
                                lam_k2[l][None, :], subln_g[l][None, :], lambda_init)
        mix_f = _fox_attention(kf, qf, xq, vf)
        wo = w_out[l].astype(bf16)
        y = _post(mix_d.reshape(B * S, dw), mix_f.reshape(B * S, fw), x.reshape(B * S, D),
                  wo[:dw], wo[dw:], w_up[l].astype(bf16), w_down[l].astype(bf16),
                  ln_attn_post[l][None, :], ln_mlp_pre[l][None, :], ln_mlp_post[l][None, :])
        x = y.reshape(B, S, D)
    return x
```

```python
import functools
import math

import numpy as np
import jax
import jax.numpy as jnp
from jax import lax
from jax.experimental import pallas as pl
from jax.experimental.pallas import tpu as pltpu

D_MODEL = 1024
DIFF_HEADS = 4
DIFF_HEAD_DIM = 64
DIFF_WIDTH = DIFF_HEADS * 2 * DIFF_HEAD_DIM
FOX_HEADS = 8
FOX_HEAD_DIM = 64
FOX_WIDTH = FOX_HEADS * FOX_HEAD_DIM
FOX_UNITS = FOX_HEADS // 2
D_FF = 4 * D_MODEL
N_BUCKETS = 32
MAX_DISTANCE = 128
EPS = 1e-6
NEG = -1e30
QK_SCALE = DIFF_HEAD_DIM ** -0.5
LOG2E = math.log2(math.e)

TQ = 256
TK = 256
TM_PROJ = 1024
TM_POST = 512
FF_CHUNK = 1024
LOOKAHEAD = 4
ONES_ROWS = 16
DV_ROWS = 2 * DIFF_HEAD_DIM + ONES_ROWS
FV_ROWS = FOX_HEAD_DIM + ONES_ROWS
XROWS = 16
VMEM_LIMIT = 56 * 1024 * 1024


def _rms(x, g):
    ms = jnp.mean(x * x, axis=-1, keepdims=True)
    return x * lax.rsqrt(ms + EPS) * g


def _proj_kernel(x_ref, g_ref, wrow_ref, wcol_ref, wf_ref, bf_ref, pq_ref, pk_ref, oq_ref, ok_ref,
                 kd_ref, kf_ref, qd_ref, vd_ref, qf_ref, vf_ref, xq_ref, carry_ref):
    tm = x_ref.shape[1]
    nsub = tm // TQ
    nt = (((1,), (1,)), ((), ()))
    ones = jnp.ones((ONES_ROWS, TQ), jnp.bfloat16)
    outs = ((qd_ref, None, None), (vd_ref, 2 * DIFF_HEAD_DIM, DV_ROWS),
            (qf_ref, None, None), (vf_ref, FOX_HEAD_DIM, FV_ROWS))

    def column_outputs(c, h, n):
        o_ref, hrows, orows = outs[n]
        colt = lax.dot_general(wcol_ref[n * 512:(n + 1) * 512, :], h, nt,
                               preferred_element_type=jnp.float32)
        if hrows is None:
            o_ref[0, c] = (colt * (QK_SCALE * LOG2E)).astype(jnp.bfloat16)
        else:
            colt = colt.astype(jnp.bfloat16)
            for hd in range(512 // hrows):
                o_ref[0, c, orows * hd:orows * hd + hrows] = colt[hrows * hd:hrows * (hd + 1)]
                o_ref[0, c, orows * hd + hrows:orows * (hd + 1)] = ones

    ft_parts = []
    for c in range(nsub):
        rows = slice(c * TQ, (c + 1) * TQ)
        h = _rms(x_ref[0, rows], g_ref[...]).astype(jnp.bfloat16)
        ft_parts.append(lax.dot_general(wf_ref[...], h, nt, preferred_element_type=jnp.float32))
        krow = jnp.dot(h, wrow_ref[...], preferred_element_type=jnp.float32)
        kd_ref[0, rows] = krow[:, :DIFF_WIDTH].astype(jnp.bfloat16)
        for u in range(FOX_UNITS):
            kf_ref[0, rows, 256 * u:256 * u + 128] = (
                krow[:, DIFF_WIDTH + 128 * u:DIFF_WIDTH + 128 * (u + 1)].astype(jnp.bfloat16))
        for n in range(len(outs)):
            column_outputs(c, h, n)

    ft = jnp.concatenate(ft_parts, axis=1) + bf_ref[...]
    ls = jnp.minimum(ft, 0.0) - jnp.log1p(jnp.exp(-jnp.abs(ft)))
    lane = lax.broadcasted_iota(jnp.int32, ls.shape, 1)
    sh = 1
    while sh < tm:
        ls = ls + jnp.where(lane >= sh, pltpu.roll(ls, sh, axis=1), 0.0)
        sh *= 2

    @pl.when(pl.program_id(1) == 0)
    def _():
        carry_ref[...] = jnp.zeros_like(carry_ref)

    cum = ls + carry_ref[...]
    carry_ref[...] = cum[:, tm - 1:tm]
    cum = cum * LOG2E

    hi = cum.astype(jnp.bfloat16)
    r1 = cum - hi.astype(jnp.float32)
    mid = r1.astype(jnp.bfloat16)
    lo = (r1 - mid.astype(jnp.float32)).astype(jnp.bfloat16)
    parts = jnp.concatenate([hi, mid, lo, jnp.zeros((128 - 3 * XROWS, tm), jnp.bfloat16)], axis=0)
    xq = jnp.dot(pq_ref[...], parts, preferred_element_type=jnp.float32) + oq_ref[...]
    xq = xq.astype(jnp.bfloat16)
    for c in range(nsub):
        xq_ref[0, c] = xq[:, c * TQ:(c + 1) * TQ]
    xk = jnp.dot(pk_ref[...], parts, preferred_element_type=jnp.float32) + ok_ref[...]
    for u in range(FOX_UNITS):
        kf_ref[0, :, 256 * u + 128:256 * (u + 1)] = (
            xk[128 * u:128 * (u + 1), :].T.astype(jnp.bfloat16))


def _placement_constants():
    pq = np.zeros((FOX_UNITS * 2 * XROWS, 128), np.float32)
    oq = np.zeros((FOX_UNITS * 2 * XROWS, 1), np.float32)
    pk = np.zeros((FOX_UNITS * 128, 128), np.float32)
    ok = np.zeros((FOX_UNITS * 128, 1), np.float32)
    for u in range(FOX_UNITS):
        for a in range(2):
            head = 2 * u + a
            for r in range(3):
                pq[2 * XROWS * u + XROWS * a + r, XROWS * r + head] = 1.0
                oq[2 * XROWS * u + XROWS * a + 3 + r, 0] = 1.0
                pk[128 * u + XROWS * a + 3 + r, XROWS * r + head] = -1.0
                ok[128 * u + XROWS * a + r, 0] = 1.0
    return (jnp.asarray(pq, jnp.bfloat16), jnp.asarray(pk, jnp.bfloat16),
            jnp.asarray(oq), jnp.asarray(ok))


def _proj(x, g, wrow, wcol, wf, bf_col):
    B, S, D = x.shape
    tm = TM_PROJ
    nsub = tm // TQ
    nq = S // TQ
    pq, pk, oq, ok = _placement_constants()
    const = lambda shape: pl.BlockSpec(shape, lambda b, t: (0,) * len(shape))
    colspec = lambda rows: pl.BlockSpec((1, nsub, rows, TQ), lambda b, t: (b, t, 0, 0))
    bf16 = jnp.bfloat16
    return pl.pallas_call(
        _proj_kernel,
        grid=(B, S // tm),
        in_specs=[
            pl.BlockSpec((1, tm, D), lambda b, t: (b, t, 0)),
            const((1, D)), const(wrow.shape), const(wcol.shape), const(wf.shape), const(bf_col.shape),
            const(pq.shape), const(pk.shape), const(oq.shape), const(ok.shape),
        ],
        out_specs=[
            pl.BlockSpec((1, tm, DIFF_WIDTH), lambda b, t: (b, t, 0)),
            pl.BlockSpec((1, tm, 2 * FOX_WIDTH), lambda b, t: (b, t, 0)),
            colspec(512), colspec(DIFF_HEADS * DV_ROWS), colspec(512), colspec(FOX_HEADS * FV_ROWS),
            colspec(FOX_UNITS * 2 * XROWS),
        ],
        out_shape=[
            jax.ShapeDtypeStruct((B, S, DIFF_WIDTH), bf16),
            jax.ShapeDtypeStruct((B, S, 2 * FOX_WIDTH), bf16),
            jax.ShapeDtypeStruct((B, nq, 512, TQ), bf16),
            jax.ShapeDtypeStruct((B, nq, DIFF_HEADS * DV_ROWS, TQ), bf16),
            jax.ShapeDtypeStruct((B, nq, 512, TQ), bf16),
            jax.ShapeDtypeStruct((B, nq, FOX_HEADS * FV_ROWS, TQ), bf16),
            jax.ShapeDtypeStruct((B, nq, FOX_UNITS * 2 * XROWS, TQ), bf16),
        ],
        scratch_shapes=[pltpu.VMEM((XROWS, 1), jnp.float32)],
        compiler_params=pltpu.CompilerParams(
            dimension_semantics=("arbitrary", "arbitrary"), vmem_limit_bytes=VMEM_LIMIT),
        name="proj",
    )(x, g, wrow, wcol, wf, bf_col, pq, pk, oq, ok)


def _t5_buckets(n):
    max_exact = N_BUCKETS // 2
    dist = np.arange(n)
    d = np.maximum(dist, 1).astype(np.float32)
    large = np.float32(max_exact) + (np.log(d / np.float32(max_exact))
                                     / np.float32(math.log(MAX_DISTANCE / max_exact))
                                     * np.float32(N_BUCKETS - max_exact))
    large = np.minimum(large.astype(np.int32), N_BUCKETS - 1)
    return np.where(dist < max_exact, dist, large).astype(np.int32)


def _bias_kernel(rel_ref, idx_ref, out_ref):
    hd = pl.program_id(0)
    idx = idx_ref[...]
    val = jnp.full(idx.shape, NEG, jnp.float32)
    for bkt in range(N_BUCKETS):
        val = jnp.where(idx == bkt, rel_ref[bkt, hd] * LOG2E, val)
    out_ref[0] = val


def _bias_tiles(rel_bias, S):
    buckets = _t5_buckets(S)
    far = int(buckets[-1])
    assert np.all(buckets[TQ + 1:] == far)
    s_loc = np.arange(TK)[:, None]
    t_loc = np.arange(TQ)[None, :]
    rel0 = t_loc - s_loc
    idx0 = np.where(rel0 >= 0, buckets[np.maximum(rel0, 0)], -1)
    idx1 = buckets[TQ + t_loc - s_loc]
    idx = jnp.asarray(np.concatenate([idx1, idx0], axis=0).astype(np.int32))
    tiles = pl.pallas_call(
        _bias_kernel,
        grid=(DIFF_HEADS,),
        in_specs=[pl.BlockSpec(memory_space=pltpu.SMEM),
                  pl.BlockSpec((2 * TK, TQ), lambda h: (0, 0))],
        out_specs=pl.BlockSpec((1, 2 * TK, TQ), lambda h: (h, 0, 0)),
        out_shape=jax.ShapeDtypeStruct((DIFF_HEADS, 2 * TK, TQ), jnp.float32),
        name="t5_bias_tiles",
    )(rel_bias, idx)
    return tiles, far


def _flash_update(m_ref, acc_ref, a, st, vt, shift=None):
    m_old = m_ref[a]
    mx = jnp.max(st, axis=0, keepdims=True)
    if shift is None:
        m_new = jnp.maximum(m_old, mx)
        x = st - m_new
    else:
        m_new = jnp.maximum(m_old, mx + shift)
        x = st - (m_new - shift)
    p = jnp.exp2(x.astype(jnp.bfloat16))
    alpha = jnp.exp2(m_old - m_new)
    acc_ref[a] = alpha * acc_ref[a] + jnp.dot(vt, p, preferred_element_type=jnp.float32)
    m_ref[a] = m_new


def _flash_init(m_ref, acc_ref):
    m_ref[...] = jnp.full(m_ref.shape, NEG, jnp.float32)
    acc_ref[...] = jnp.zeros_like(acc_ref)


def _run_chains(n, st_ref, score_fn, update_fn, j, rows, prefetch_j):
    pending = [(c, st_ref[c, :rows]) for c in range(LOOKAHEAD)]
    for c in range(LOOKAHEAD, n):
        pending.append((c, score_fn(c, j, rows)))
        update_fn(*pending.pop(0))
    for c in range(LOOKAHEAD):
        if prefetch_j is not None:
            st_ref[c] = score_fn(c, prefetch_j, 2 * TK)
        update_fn(*pending.pop(0))


def _value_rows(v_ref, j, rows, r0, nrows):
    blocks = [v_ref[0, j + b, r0:r0 + nrows] for b in range(rows // TK)]
    return blocks[0] if len(blocks) == 1 else jnp.concatenate(blocks, axis=1)


def _stage_keys(k_ref, kbuf_ref, qpad_ref, first):
    @pl.when(first)
    def _():
        def body(t, carry):
            r = pl.multiple_of(t * TK, TK)
            kbuf_ref[pl.ds(r, TK), :] = k_ref[0, pl.ds(r, TK), :]
            return carry

        lax.fori_loop(0, k_ref.shape[1] // TK, body, 0)
        qpad_ref[...] = jnp.zeros_like(qpad_ref)


def _prime_chains(st_ref, score_fn):
    for c in range(LOOKAHEAD):
        st_ref[c] = score_fn(c, 0, 2 * TK)


def _key_block_schedule(i, step):
    nfar = jnp.maximum(i - 1, 0)

    def far_pair(jj, carry):
        step(2 * jj, 2 * TK, "far", 2 * jj + 2)
        return carry

    lax.fori_loop(0, lax.shift_right_logical(nfar, 1), far_pair, 0)

    @pl.when((nfar & 1) == 1)
    def _():
        step(i - 2, TK, "far", i - 1)

    @pl.when(i >= 1)
    def _():
        step(i - 1, 2 * TK, "last", None)

    @pl.when(i == 0)
    def _():
        step(0, TK, "diag_only", None)


def _diff_kernel(far_bucket, lambda_init,
                 rel_ref, k_ref, q_ref, v_ref, bias_ref, lq1_ref, lk1_ref, lq2_ref, lk2_ref, g_ref,
                 o_ref, qpad_ref, m_ref, acc_ref, st_ref, kbuf_ref):
    i = pl.program_id(1)
    d = DIFF_HEAD_DIM
    _stage_keys(k_ref, kbuf_ref, qpad_ref, i == 0)
    for c in range(2 * DIFF_HEADS):
        r0 = d * (c % 4)
        qpad_ref[c, r0:r0 + d] = q_ref[0, 0, d * c:d * (c + 1)]
    _flash_init(m_ref, acc_ref)

    def score(c, j, rows):
        pair = c // 4
        kblk = kbuf_ref[pl.ds(pl.multiple_of(j * TK, TK), rows), 256 * pair:256 * (pair + 1)]
        return jnp.dot(kblk, qpad_ref[c], preferred_element_type=jnp.float32)

    def step(j, rows, kind, prefetch_j):
        def update(c, st):
            hd = c // 2
            vt = _value_rows(v_ref, j, rows, DV_ROWS * hd, DV_ROWS)
            if kind == "far":
                _flash_update(m_ref, acc_ref, c, st, vt, shift=rel_ref[far_bucket, hd] * LOG2E)
            else:
                bias = bias_ref[hd] if kind == "last" else bias_ref[hd, TK:]
                _flash_update(m_ref, acc_ref, c, bias + st, vt)

        _run_chains(2 * DIFF_HEADS, st_ref, score, update, j, rows, prefetch_j)

    _prime_chains(st_ref, score)
    _key_block_schedule(i, step)

    lam = (jnp.exp(jnp.sum(lq1_ref[...] * lk1_ref[...], axis=-1, keepdims=True))
           - jnp.exp(jnp.sum(lq2_ref[...] * lk2_ref[...], axis=-1, keepdims=True))
           + lambda_init)
    for hd in range(DIFF_HEADS):
        dv = 2 * d
        o0 = acc_ref[2 * hd, :dv] * (1.0 / acc_ref[2 * hd, dv:dv + 1])
        o1 = acc_ref[2 * hd + 1, :dv] * (1.0 / acc_ref[2 * hd + 1, dv:dv + 1])
        od = o0 - lam * o1
        ms = jnp.mean(od * od, axis=0, keepdims=True)
        y = (od * lax.rsqrt(ms + EPS)).T * g_ref[...]
        o_ref[0, :, 128 * hd:128 * (hd + 1)] = (y * (1.0 - lambda_init)).astype(o_ref.dtype)


def _diff_attention(kd, qd, vd, bias_tiles, far_bucket, rel_bias, lq1, lk1, lq2, lk2, g, lambda_init):
    B, S, _ = kd.shape
    nq = S // TQ
    nmaps = 2 * DIFF_HEADS
    vec = lambda n: pl.BlockSpec((1, n), lambda b, i: (0, 0))
    return pl.pallas_call(
        functools.partial(_diff_kernel, far_bucket, lambda_init),
        grid=(B, nq),
        in_specs=[
            pl.BlockSpec(memory_space=pltpu.SMEM),
            pl.BlockSpec((1, S, DIFF_WIDTH), lambda b, i: (b, 0, 0)),
            pl.BlockSpec((1, 1, DIFF_WIDTH, TQ), lambda b, i: (b, i, 0, 0)),
            pl.BlockSpec((1, nq, DIFF_HEADS * DV_ROWS, TK), lambda b, i: (b, 0, 0, 0)),
            pl.BlockSpec((DIFF_HEADS, 2 * TK, TQ), lambda b, i: (0, 0, 0)),
            vec(DIFF_HEAD_DIM), vec(DIFF_HEAD_DIM), vec(DIFF_HEAD_DIM), vec(DIFF_HEAD_DIM),
            vec(2 * DIFF_HEAD_DIM),
        ],
        out_specs=pl.BlockSpec((1, TQ, DIFF_WIDTH), lambda b, i: (b, i, 0)),
        out_shape=jax.ShapeDtypeStruct((B, S, DIFF_WIDTH), jnp.bfloat16),
        scratch_shapes=[
            pltpu.VMEM((nmaps, 256, TQ), jnp.bfloat16),
            pltpu.VMEM((nmaps, 1, TQ), jnp.float32),
            pltpu.VMEM((nmaps, DV_ROWS, TQ), jnp.float32),
            pltpu.VMEM((LOOKAHEAD, 2 * TK, TQ), jnp.float32),
            pltpu.VMEM((S, DIFF_WIDTH), jnp.bfloat16),
        ],
        compiler_params=pltpu.CompilerParams(
            dimension_semantics=("parallel", "arbitrary"), vmem_limit_bytes=VMEM_LIMIT),
        name="diff_attention",
    )(rel_bias, kd, qd, vd, bias_tiles, lq1, lk1, lq2, lk2, g)


def _fox_kernel(k_ref, q_ref, xq_ref, v_ref, o_ref, qpad_ref, m_ref, acc_ref, st_ref, kbuf_ref):
    i = pl.program_id(1)
    d = FOX_HEAD_DIM
    _stage_keys(k_ref, kbuf_ref, qpad_ref, i == 0)
    for hd in range(FOX_HEADS):
        a = hd % 2
        qpad_ref[hd, d * a:d * (a + 1)] = q_ref[0, 0, d * hd:d * (hd + 1)]
        qpad_ref[hd, 128 + XROWS * a:128 + XROWS * (a + 1)] = xq_ref[0, 0, XROWS * hd:XROWS * (hd + 1)]
    _flash_init(m_ref, acc_ref)

    def score(hd, j, rows):
        u = hd // 2
        kblk = kbuf_ref[pl.ds(pl.multiple_of(j * TK, TK), rows), 256 * u:256 * (u + 1)]
        return jnp.dot(kblk, qpad_ref[hd], preferred_element_type=jnp.float32)

    def step(j, rows, kind, prefetch_j):
        def update(hd, st):
            if kind != "far":
                row = lax.broadcasted_iota(jnp.int32, st.shape, 0) - (rows - TK)
                col = lax.broadcasted_iota(jnp.int32, st.shape, 1)
                st = jnp.where(row > col, NEG, st)
            _flash_update(m_ref, acc_ref, hd, st, _value_rows(v_ref, j, rows, FV_ROWS * hd, FV_ROWS))

        _run_chains(FOX_HEADS, st_ref, score, update, j, rows, prefetch_j)

    _prime_chains(st_ref, score)
    _key_block_schedule(i, step)

    for u in range(FOX_UNITS):
        o = jnp.concatenate([acc_ref[hd, :d] * (1.0 / acc_ref[hd, d:d + 1]) for hd in (2 * u, 2 * u + 1)],
                            axis=0)
        o_ref[0, :, 128 * u:128 * (u + 1)] = o.T.astype(o_ref.dtype)


def _fox_attention(kf, qf, xq, vf):
    B, S, _ = kf.shape
    nq = S // TQ
    return pl.pallas_call(
        _fox_kernel,
        grid=(B, nq),
        in_specs=[
            pl.BlockSpec((1, S, 2 * FOX_WIDTH), lambda b, i: (b, 0, 0)),
            pl.BlockSpec((1, 1, FOX_WIDTH, TQ), lambda b, i: (b, i, 0, 0)),
            pl.BlockSpec((1, 1, FOX_HEADS * XROWS, TQ), lambda b, i: (b, i, 0, 0)),
            pl.BlockSpec((1, nq, FOX_HEADS * FV_ROWS, TK), lambda b, i: (b, 0, 0, 0)),
        ],
        out_specs=pl.BlockSpec((1, TQ, FOX_WIDTH), lambda b, i: (b, i, 0)),
        out_shape=jax.ShapeDtypeStruct((B, S, FOX_WIDTH), jnp.bfloat16),
        scratch_shapes=[
            pltpu.VMEM((FOX_HEADS, 256, TQ), jnp.bfloat16),
            pltpu.VMEM((FOX_HEADS, 1, TQ), jnp.float32),
            pltpu.VMEM((FOX_HEADS, FV_ROWS, TQ), jnp.float32),
            pltpu.VMEM((LOOKAHEAD, 2 * TK, TQ), jnp.float32),
            pltpu.VMEM((S, 2 * FOX_WIDTH), jnp.bfloat16),
        ],
        compiler_params=pltpu.CompilerParams(
            dimension_semantics=("parallel", "arbitrary"), vmem_limit_bytes=VMEM_LIMIT),
        name="fox_attention",
    )(kf, qf, xq, vf)


def _post_kernel(md_ref, mf_ref, x_ref, wod_ref, wof_ref, wup_ref, wdn_ref, g1_ref, g2_ref, g3_ref,
                 o_ref):
    a = (jnp.dot(md_ref[...], wod_ref[...], preferred_element_type=jnp.float32)
         + jnp.dot(mf_ref[...], wof_ref[...], preferred_element_type=jnp.float32))
    x1 = x_ref[...] + _rms(a, g1_ref[...])
    h2 = _rms(x1, g2_ref[...]).astype(jnp.bfloat16)
    acc = None
    for c in range(D_FF // FF_CHUNK):
        u = jnp.dot(h2, wup_ref[:, c * FF_CHUNK:(c + 1) * FF_CHUNK], preferred_element_type=jnp.float32)
        u = jnp.square(jnp.maximum(u, 0.0)).astype(jnp.bfloat16)
        part = jnp.dot(u, wdn_ref[c * FF_CHUNK:(c + 1) * FF_CHUNK, :], preferred_element_type=jnp.float32)
        acc = part if acc is None else acc + part
    o_ref[...] = x1 + _rms(acc, g3_ref[...])


def _post(mix_d, mix_f, x2d, wod, wof, wup, wdn, g1, g2, g3):
    R, D = x2d.shape
    tm = TM_POST
    resident = lambda shape: pl.BlockSpec(shape, lambda r: (0,) * len(shape),
                                          pipeline_mode=pl.Buffered(1))
    return pl.pallas_call(
        _post_kernel,
        grid=(R // tm,),
        in_specs=[
            pl.BlockSpec((tm, DIFF_WIDTH), lambda r: (r, 0)),
            pl.BlockSpec((tm, FOX_WIDTH), lambda r: (r, 0)),
            pl.BlockSpec((tm, D), lambda r: (r, 0)),
            resident(wod.shape), resident(wof.shape), resident(wup.shape), resident(wdn.shape),
            resident((1, D)), resident((1, D)), resident((1, D)),
        ],
        out_specs=pl.BlockSpec((tm, D), lambda r: (r, 0)),
        out_shape=jax.ShapeDtypeStruct((R, D), jnp.float32),
        compiler_params=pltpu.CompilerParams(
            dimension_semantics=("parallel",), vmem_limit_bytes=VMEM_LIMIT),
        name="post",
    )(mix_d, mix_f, x2d, wod, wof, wup, wdn, g1, g2, g3)


def kernel(x, ln_attn_pre, w_in, b_f, lam_q1, lam_k1, lam_q2, lam_k2, subln_g, rel_bias,
           w_out, ln_attn_post, ln_mlp_pre, w_up, w_down, ln_mlp_post):
    B, S, D = x.shape
    depth = w_in.shape[0]
    bf16 = jnp.bfloat16
    bias_tiles, far_bucket = _bias_tiles(rel_bias, S)
    dw, fw = DIFF_WIDTH, FOX_WIDTH
    for l in range(depth):
        lambda_init = 0.8 - 0.6 * math.exp(-0.3 * l)
        w = w_in[l]
        o_dq, o_dk, o_dv, o_fq, o_fk, o_fv, o_g = 0, dw, 2 * dw, 3 * dw, 3 * dw + fw, 3 * dw + 2 * fw, 3 * dw + 3 * fw
        wrow = jnp.concatenate([w[:, o_dk:o_dk + dw], w[:, o_fk:o_fk + fw]], axis=1).astype(bf16)
        wcol = jnp.concatenate([w[:, o_dq:o_dq + dw], w[:, o_dv:o_dv + dw],
                                w[:, o_fq:o_fq + fw], w[:, o_fv:o_fv + fw]], axis=1).T.astype(bf16)
        wf = jnp.zeros((XROWS, D), bf16).at[:FOX_HEADS].set(w[:, o_g:o_g + FOX_HEADS].T.astype(bf16))
        bf_col = jnp.zeros((XROWS, 1), jnp.float32).at[:FOX_HEADS, 0].set(b_f[l].astype(jnp.float32))

        kd, kf, qd, vd, qf, vf, xq = _proj(x, ln_attn_pre[l][None, :], wrow, wcol, wf, bf_col)
        mix_d = _diff_attention(kd, qd, vd, bias_tiles, far_bucket, rel_bias,
                                lam_q1[l][None, :], lam_k1[l][None, :], lam_q2[l][None, :],
                                lam_k2[l][None, :], subln_g[l][None, :], lambda_init)
        mix_f = _fox_attention(kf, qf, xq, vf)
        wo = w_out[l].astype(bf16)
        y = _post(mix_d.reshape(B * S, dw), mix_f.reshape(B * S, fw), x.reshape(B * S, D),
                  wo[:dw], wo[dw:], w_up[l].astype(bf16), w_down[l].astype(bf16),
                  ln_attn_post[l][None, :], ln_mlp_pre[l][None, :], ln_mlp_post[l][None, :])
        x = y.reshape(B, S, D)
    return x
```

```python
import functools
import math

import numpy as np
import jax
import jax.numpy as jnp
from jax import lax
from jax.experimental import pallas as pl
from jax.experimental.pallas import tpu as pltpu

D_MODEL = 1024
DIFF_HEADS = 4
DIFF_HEAD_DIM = 64
DIFF_WIDTH = DIFF_HEADS * 2 * DIFF_HEAD_DIM
FOX_HEADS = 8
FOX_HEAD_DIM = 64
FOX_WIDTH = FOX_HEADS * FOX_HEAD_DIM
FOX_UNITS = FOX_HEADS // 2
D_FF = 4 * D_MODEL
N_BUCKETS = 32
MAX_DISTANCE = 128
EPS = 1e-6
NEG = -1e30
QK_SCALE = DIFF_HEAD_DIM ** -0.5
LOG2E = math.log2(math.e)

TQ = 256
TK = 256
TM_PROJ = 1024
TM_POST = 1024
POST_SUB = 512
FF_CHUNK = 1024
LOOKAHEAD = 4
ONES_ROWS = 16
DV_ROWS = 2 * DIFF_HEAD_DIM + ONES_ROWS
FV_ROWS = FOX_HEAD_DIM + ONES_ROWS
XROWS = 16
VMEM_LIMIT = 56 * 1024 * 1024


def _rms(x, g):
    ms = jnp.mean(x * x, axis=-1, keepdims=True)
    return x * lax.rsqrt(ms + EPS) * g


def _proj_kernel(x_ref, g_ref, wrow_ref, wcol_ref, wf_ref, bf_ref, pq_ref, pk_ref, oq_ref, ok_ref,
                 kd_ref, kf_ref, qd_ref, vd_ref, qf_ref, vf_ref, xq_ref, carry_ref):
    tm = x_ref.shape[1]
    nsub = tm // TQ
    nt = (((1,), (1,)), ((), ()))
    ones = jnp.ones((ONES_ROWS, TQ), jnp.bfloat16)
    outs = ((qd_ref, None, None), (vd_ref, 2 * DIFF_HEAD_DIM, DV_ROWS),
            (qf_ref, None, None), (vf_ref, FOX_HEAD_DIM, FV_ROWS))

    def column_outputs(c, h, n):
        o_ref, hrows, orows = outs[n]
        colt = lax.dot_general(wcol_ref[n * 512:(n + 1) * 512, :], h, nt,
                               preferred_element_type=jnp.float32)
        if hrows is None:
            o_ref[0, c] = (colt * (QK_SCALE * LOG2E)).astype(jnp.bfloat16)
        else:
            colt = colt.astype(jnp.bfloat16)
            for hd in range(512 // hrows):
                o_ref[0, c, orows * hd:orows * hd + hrows] = colt[hrows * hd:hrows * (hd + 1)]
                o_ref[0, c, orows * hd + hrows:orows * (hd + 1)] = ones

    ft_parts = []
    for c in range(nsub):
        rows = slice(c * TQ, (c + 1) * TQ)
        h = _rms(x_ref[0, rows], g_ref[...]).astype(jnp.bfloat16)
        ft_parts.append(lax.dot_general(wf_ref[...], h, nt, preferred_element_type=jnp.float32))
        krow = jnp.dot(h, wrow_ref[...], preferred_element_type=jnp.float32)
        kd_ref[0, rows] = krow[:, :DIFF_WIDTH].astype(jnp.bfloat16)
        for u in range(FOX_UNITS):
            kf_ref[0, rows, 256 * u:256 * u + 128] = (
                krow[:, DIFF_WIDTH + 128 * u:DIFF_WIDTH + 128 * (u + 1)].astype(jnp.bfloat16))
        for n in range(len(outs)):
            column_outputs(c, h, n)

    ft = jnp.concatenate(ft_parts, axis=1) + bf_ref[...]
    ls = jnp.minimum(ft, 0.0) - jnp.log1p(jnp.exp(-jnp.abs(ft)))
    lane = lax.broadcasted_iota(jnp.int32, ls.shape, 1)
    sh = 1
    while sh < tm:
        ls = ls + jnp.where(lane >= sh, pltpu.roll(ls, sh, axis=1), 0.0)
        sh *= 2

    @pl.when(pl.program_id(1) == 0)
    def _():
        carry_ref[...] = jnp.zeros_like(carry_ref)

    cum = ls + carry_ref[...]
    carry_ref[...] = cum[:, tm - 1:tm]
    cum = cum * LOG2E

    hi = cum.astype(jnp.bfloat16)
    r1 = cum - hi.astype(jnp.float32)
    mid = r1.astype(jnp.bfloat16)
    lo = (r1 - mid.astype(jnp.float32)).astype(jnp.bfloat16)
    parts = jnp.concatenate([hi, mid, lo, jnp.zeros((128 - 3 * XROWS, tm), jnp.bfloat16)], axis=0)
    xq = jnp.dot(pq_ref[...], parts, preferred_element_type=jnp.float32) + oq_ref[...]
    xq = xq.astype(jnp.bfloat16)
    for c in range(nsub):
        xq_ref[0, c] = xq[:, c * TQ:(c + 1) * TQ]
    xk = jnp.dot(pk_ref[...], parts, preferred_element_type=jnp.float32) + ok_ref[...]
    for u in range(FOX_UNITS):
        kf_ref[0, :, 256 * u + 128:256 * (u + 1)] = (
            xk[128 * u:128 * (u + 1), :].T.astype(jnp.bfloat16))


def _placement_constants():
    pq = np.zeros((FOX_UNITS * 2 * XROWS, 128), np.float32)
    oq = np.zeros((FOX_UNITS * 2 * XROWS, 1), np.float32)
    pk = np.zeros((FOX_UNITS * 128, 128), np.float32)
    ok = np.zeros((FOX_UNITS * 128, 1), np.float32)
    for u in range(FOX_UNITS):
        for a in range(2):
            head = 2 * u + a
            for r in range(3):
                pq[2 * XROWS * u + XROWS * a + r, XROWS * r + head] = 1.0
                oq[2 * XROWS * u + XROWS * a + 3 + r, 0] = 1.0
                pk[128 * u + XROWS * a + 3 + r, XROWS * r + head] = -1.0
                ok[128 * u + XROWS * a + r, 0] = 1.0
    return (jnp.asarray(pq, jnp.bfloat16), jnp.asarray(pk, jnp.bfloat16),
            jnp.asarray(oq), jnp.asarray(ok))


def _proj(x, g, wrow, wcol, wf, bf_col):
    B, S, D = x.shape
    tm = TM_PROJ
    nsub = tm // TQ
    nq = S // TQ
    pq, pk, oq, ok = _placement_constants()
    const = lambda shape: pl.BlockSpec(shape, lambda b, t: (0,) * len(shape))
    colspec = lambda rows: pl.BlockSpec((1, nsub, rows, TQ), lambda b, t: (b, t, 0, 0))
    bf16 = jnp.bfloat16
    return pl.pallas_call(
        _proj_kernel,
        grid=(B, S // tm),
        in_specs=[
            pl.BlockSpec((1, tm, D), lambda b, t: (b, t, 0)),
            const((1, D)), const(wrow.shape), const(wcol.shape), const(wf.shape), const(bf_col.shape),
            const(pq.shape), const(pk.shape), const(oq.shape), const(ok.shape),
        ],
        out_specs=[
            pl.BlockSpec((1, tm, DIFF_WIDTH), lambda b, t: (b, t, 0)),
            pl.BlockSpec((1, tm, 2 * FOX_WIDTH), lambda b, t: (b, t, 0)),
            colspec(512), colspec(DIFF_HEADS * DV_ROWS), colspec(512), colspec(FOX_HEADS * FV_ROWS),
            colspec(FOX_UNITS * 2 * XROWS),
        ],
        out_shape=[
            jax.ShapeDtypeStruct((B, S, DIFF_WIDTH), bf16),
            jax.ShapeDtypeStruct((B, S, 2 * FOX_WIDTH), bf16),
            jax.ShapeDtypeStruct((B, nq, 512, TQ), bf16),
            jax.ShapeDtypeStruct((B, nq, DIFF_HEADS * DV_ROWS, TQ), bf16),
            jax.ShapeDtypeStruct((B, nq, 512, TQ), bf16),
            jax.ShapeDtypeStruct((B, nq, FOX_HEADS * FV_ROWS, TQ), bf16),
            jax.ShapeDtypeStruct((B, nq, FOX_UNITS * 2 * XROWS, TQ), bf16),
        ],
        scratch_shapes=[pltpu.VMEM((XROWS, 1), jnp.float32)],
        compiler_params=pltpu.CompilerParams(
            dimension_semantics=("arbitrary", "arbitrary"), vmem_limit_bytes=VMEM_LIMIT),
        name="proj",
    )(x, g, wrow, wcol, wf, bf_col, pq, pk, oq, ok)


def _t5_buckets(n):
    max_exact = N_BUCKETS // 2
    dist = np.arange(n)
    d = np.maximum(dist, 1).astype(np.float32)
    large = np.float32(max_exact) + (np.log(d / np.float32(max_exact))
                                     / np.float32(math.log(MAX_DISTANCE / max_exact))
                                     * np.float32(N_BUCKETS - max_exact))
    large = np.minimum(large.astype(np.int32), N_BUCKETS - 1)
    return np.where(dist < max_exact, dist, large).astype(np.int32)


def _bias_kernel(rel_ref, idx_ref, out_ref):
    hd = pl.program_id(0)
    idx = idx_ref[...]
    val = jnp.full(idx.shape, NEG, jnp.float32)
    for bkt in range(N_BUCKETS):
        val = jnp.where(idx == bkt, rel_ref[bkt, hd] * LOG2E, val)
    out_ref[0] = val


def _bias_tiles(rel_bias, S):
    buckets = _t5_buckets(S)
    far = int(buckets[-1])
    assert np.all(buckets[TQ + 1:] == far)
    s_loc = np.arange(TK)[:, None]
    t_loc = np.arange(TQ)[None, :]
    rel0 = t_loc - s_loc
    idx0 = np.where(rel0 >= 0, buckets[np.maximum(rel0, 0)], -1)
    idx1 = buckets[TQ + t_loc - s_loc]
    idx = jnp.asarray(np.concatenate([idx1, idx0], axis=0).astype(np.int32))
    tiles = pl.pallas_call(
        _bias_kernel,
        grid=(DIFF_HEADS,),
        in_specs=[pl.BlockSpec(memory_space=pltpu.SMEM),
                  pl.BlockSpec((2 * TK, TQ), lambda h: (0, 0))],
        out_specs=pl.BlockSpec((1, 2 * TK, TQ), lambda h: (h, 0, 0)),
        out_shape=jax.ShapeDtypeStruct((DIFF_HEADS, 2 * TK, TQ), jnp.float32),
        name="t5_bias_tiles",
    )(rel_bias, idx)
    return tiles, far


def _flash_update(m_ref, acc_ref, a, st, vt, shift=None):
    m_old = m_ref[a]
    mx = jnp.max(st, axis=0, keepdims=True)
    if shift is None:
        m_new = jnp.maximum(m_old, mx)
        x = st - m_new
    else:
        m_new = jnp.maximum(m_old, mx + shift)
        x = st - (m_new - shift)
    p = jnp.exp2(x.astype(jnp.bfloat16))
    alpha = jnp.exp2(m_old - m_new)
    acc_ref[a] = alpha * acc_ref[a] + jnp.dot(vt, p, preferred_element_type=jnp.float32)
    m_ref[a] = m_new


def _flash_init(m_ref, acc_ref):
    m_ref[...] = jnp.full(m_ref.shape, NEG, jnp.float32)
    acc_ref[...] = jnp.zeros_like(acc_ref)


def _run_chains(n, st_ref, score_fn, update_fn, j, rows, prefetch_j):
    pending = [(c, st_ref[c, :rows]) for c in range(LOOKAHEAD)]
    for c in range(LOOKAHEAD, n):
        pending.append((c, score_fn(c, j, rows)))
        update_fn(*pending.pop(0))
    for c in range(LOOKAHEAD):
        if prefetch_j is not None:
            st_ref[c] = score_fn(c, prefetch_j, 2 * TK)
        update_fn(*pending.pop(0))


def _value_rows(v_ref, j, rows, r0, nrows):
    blocks = [v_ref[0, j + b, r0:r0 + nrows] for b in range(rows // TK)]
    return blocks[0] if len(blocks) == 1 else jnp.concatenate(blocks, axis=1)


def _stage_keys(k_ref, kbuf_ref, qpad_ref, first):
    @pl.when(first)
    def _():
        def body(t, carry):
            r = pl.multiple_of(t * TK, TK)
            kbuf_ref[pl.ds(r, TK), :] = k_ref[0, pl.ds(r, TK), :]
            return carry

        lax.fori_loop(0, k_ref.shape[1] // TK, body, 0)
        qpad_ref[...] = jnp.zeros_like(qpad_ref)


def _prime_chains(st_ref, score_fn):
    for c in range(LOOKAHEAD):
        st_ref[c] = score_fn(c, 0, 2 * TK)


def _key_block_schedule(i, step):
    nfar = jnp.maximum(i - 1, 0)

    def far_pair(jj, carry):
        step(2 * jj, 2 * TK, "far", 2 * jj + 2)
        return carry

    lax.fori_loop(0, lax.shift_right_logical(nfar, 1), far_pair, 0)

    @pl.when((nfar & 1) == 1)
    def _():
        step(i - 2, TK, "far", i - 1)

    @pl.when(i >= 1)
    def _():
        step(i - 1, 2 * TK, "last", None)

    @pl.when(i == 0)
    def _():
        step(0, TK, "diag_only", None)


def _diff_kernel(far_bucket, lambda_init,
                 rel_ref, k_ref, q_ref, v_ref, bias_ref, lq1_ref, lk1_ref, lq2_ref, lk2_ref, g_ref,
                 o_ref, qpad_ref, m_ref, acc_ref, st_ref, kbuf_ref):
    i = pl.program_id(1)
    d = DIFF_HEAD_DIM
    _stage_keys(k_ref, kbuf_ref, qpad_ref, i == 0)
    for c in range(2 * DIFF_HEADS):
        r0 = d * (c % 4)
        qpad_ref[c, r0:r0 + d] = q_ref[0, 0, d * c:d * (c + 1)]
    _flash_init(m_ref, acc_ref)

    def score(c, j, rows):
        pair = c // 4
        kblk = kbuf_ref[pl.ds(pl.multiple_of(j * TK, TK), rows), 256 * pair:256 * (pair + 1)]
        return jnp.dot(kblk, qpad_ref[c], preferred_element_type=jnp.float32)

    def step(j, rows, kind, prefetch_j):
        def update(c, st):
            hd = c // 2
            vt = _value_rows(v_ref, j, rows, DV_ROWS * hd, DV_ROWS)
            if kind == "far":
                _flash_update(m_ref, acc_ref, c, st, vt, shift=rel_ref[far_bucket, hd] * LOG2E)
            else:
                bias = bias_ref[hd] if kind == "last" else bias_ref[hd, TK:]
                _flash_update(m_ref, acc_ref, c, bias + st, vt)

        _run_chains(2 * DIFF_HEADS, st_ref, score, update, j, rows, prefetch_j)

    _prime_chains(st_ref, score)
    _key_block_schedule(i, step)

    lam = (jnp.exp(jnp.sum(lq1_ref[...] * lk1_ref[...], axis=-1, keepdims=True))
           - jnp.exp(jnp.sum(lq2_ref[...] * lk2_ref[...], axis=-1, keepdims=True))
           + lambda_init)
    for hd in range(DIFF_HEADS):
        dv = 2 * d
        o0 = acc_ref[2 * hd, :dv] * (1.0 / acc_ref[2 * hd, dv:dv + 1])
        o1 = acc_ref[2 * hd + 1, :dv] * (1.0 / acc_ref[2 * hd + 1, dv:dv + 1])
        od = o0 - lam * o1
        ms = jnp.mean(od * od, axis=0, keepdims=True)
        y = (od * lax.rsqrt(ms + EPS)).T * g_ref[...]
        o_ref[0, :, 128 * hd:128 * (hd + 1)] = (y * (1.0 - lambda_init)).astype(o_ref.dtype)


def _diff_attention(kd, qd, vd, bias_tiles, far_bucket, rel_bias, lq1, lk1, lq2, lk2, g, lambda_init):
    B, S, _ = kd.shape
    nq = S // TQ
    nmaps = 2 * DIFF_HEADS
    vec = lambda n: pl.BlockSpec((1, n), lambda b, i: (0, 0))
    return pl.pallas_call(
        functools.partial(_diff_kernel, far_bucket, lambda_init),
        grid=(B, nq),
        in_specs=[
            pl.BlockSpec(memory_space=pltpu.SMEM),
            pl.BlockSpec((1, S, DIFF_WIDTH), lambda b, i: (b, 0, 0)),
            pl.BlockSpec((1, 1, DIFF_WIDTH, TQ), lambda b, i: (b, i, 0, 0)),
            pl.BlockSpec((1, nq, DIFF_HEADS * DV_ROWS, TK), lambda b, i: (b, 0, 0, 0)),
            pl.BlockSpec((DIFF_HEADS, 2 * TK, TQ), lambda b, i: (0, 0, 0)),
            vec(DIFF_HEAD_DIM), vec(DIFF_HEAD_DIM), vec(DIFF_HEAD_DIM), vec(DIFF_HEAD_DIM),
            vec(2 * DIFF_HEAD_DIM),
        ],
        out_specs=pl.BlockSpec((1, TQ, DIFF_WIDTH), lambda b, i: (b, i, 0)),
        out_shape=jax.ShapeDtypeStruct((B, S, DIFF_WIDTH), jnp.bfloat16),
        scratch_shapes=[
            pltpu.VMEM((nmaps, 256, TQ), jnp.bfloat16),
            pltpu.VMEM((nmaps, 1, TQ), jnp.float32),
            pltpu.VMEM((nmaps, DV_ROWS, TQ), jnp.float32),
            pltpu.VMEM((LOOKAHEAD, 2 * TK, TQ), jnp.float32),
            pltpu.VMEM((S, DIFF_WIDTH), jnp.bfloat16),
        ],
        compiler_params=pltpu.CompilerParams(
            dimension_semantics=("parallel", "arbitrary"), vmem_limit_bytes=VMEM_LIMIT),
        name="diff_attention",
    )(rel_bias, kd, qd, vd, bias_tiles, lq1, lk1, lq2, lk2, g)


def _fox_kernel(k_ref, q_ref, xq_ref, v_ref, o_ref, qpad_ref, m_ref, acc_ref, st_ref, kbuf_ref):
    i = pl.program_id(1)
    d = FOX_HEAD_DIM
    _stage_keys(k_ref, kbuf_ref, qpad_ref, i == 0)
    for hd in range(FOX_HEADS):
        a = hd % 2
        qpad_ref[hd, d * a:d * (a + 1)] = q_ref[0, 0, d * hd:d * (hd + 1)]
        qpad_ref[hd, 128 + XROWS * a:128 + XROWS * (a + 1)] = xq_ref[0, 0, XROWS * hd:XROWS * (hd + 1)]
    _flash_init(m_ref, acc_ref)

    def score(hd, j, rows):
        u = hd // 2
        kblk = kbuf_ref[pl.ds(pl.multiple_of(j * TK, TK), rows), 256 * u:256 * (u + 1)]
        return jnp.dot(kblk, qpad_ref[hd], preferred_element_type=jnp.float32)

    def step(j, rows, kind, prefetch_j):
        def update(hd, st):
            if kind != "far":
                row = lax.broadcasted_iota(jnp.int32, st.shape, 0) - (rows - TK)
                col = lax.broadcasted_iota(jnp.int32, st.shape, 1)
                st = jnp.where(row > col, NEG, st)
            _flash_update(m_ref, acc_ref, hd, st, _value_rows(v_ref, j, rows, FV_ROWS * hd, FV_ROWS))

        _run_chains(FOX_HEADS, st_ref, score, update, j, rows, prefetch_j)

    _prime_chains(st_ref, score)
    _key_block_schedule(i, step)

    for u in range(FOX_UNITS):
        o = jnp.concatenate([acc_ref[hd, :d] * (1.0 / acc_ref[hd, d:d + 1]) for hd in (2 * u, 2 * u + 1)],
                            axis=0)
        o_ref[0, :, 128 * u:128 * (u + 1)] = o.T.astype(o_ref.dtype)


def _fox_attention(kf, qf, xq, vf):
    B, S, _ = kf.shape
    nq = S // TQ
    return pl.pallas_call(
        _fox_kernel,
        grid=(B, nq),
        in_specs=[
            pl.BlockSpec((1, S, 2 * FOX_WIDTH), lambda b, i: (b, 0, 0)),
            pl.BlockSpec((1, 1, FOX_WIDTH, TQ), lambda b, i: (b, i, 0, 0)),
            pl.BlockSpec((1, 1, FOX_HEADS * XROWS, TQ), lambda b, i: (b, i, 0, 0)),
            pl.BlockSpec((1, nq, FOX_HEADS * FV_ROWS, TK), lambda b, i: (b, 0, 0, 0)),
        ],
        out_specs=pl.BlockSpec((1, TQ, FOX_WIDTH), lambda b, i: (b, i, 0)),
        out_shape=jax.ShapeDtypeStruct((B, S, FOX_WIDTH), jnp.bfloat16),
        scratch_shapes=[
            pltpu.VMEM((FOX_HEADS, 256, TQ), jnp.bfloat16),
            pltpu.VMEM((FOX_HEADS, 1, TQ), jnp.float32),
            pltpu.VMEM((FOX_HEADS, FV_ROWS, TQ), jnp.float32),
            pltpu.VMEM((LOOKAHEAD, 2 * TK, TQ), jnp.float32),
            pltpu.VMEM((S, 2 * FOX_WIDTH), jnp.bfloat16),
        ],
        compiler_params=pltpu.CompilerParams(
            dimension_semantics=("parallel", "arbitrary"), vmem_limit_bytes=VMEM_LIMIT),
        name="fox_attention",
    )(kf, qf, xq, vf)


def _post_kernel(md_ref, mf_ref, x_ref, wod_ref, wof_ref, wup_ref, wdn_ref, g1_ref, g2_ref, g3_ref,
                 o_ref):
    subs = [slice(s * POST_SUB, (s + 1) * POST_SUB) for s in range(x_ref.shape[0] // POST_SUB)]
    attn = [jnp.dot(md_ref[rows], wod_ref[...], preferred_element_type=jnp.float32)
            + jnp.dot(mf_ref[rows], wof_ref[...], preferred_element_type=jnp.float32) for rows in subs]
    for rows, a in zip(subs, attn):
        x1 = x_ref[rows] + _rms(a, g1_ref[...])
        h2 = _rms(x1, g2_ref[...]).astype(jnp.bfloat16)
        acc = None
        for c in range(D_FF // FF_CHUNK):
            u = jnp.dot(h2, wup_ref[:, c * FF_CHUNK:(c + 1) * FF_CHUNK], preferred_element_type=jnp.float32)
            u = jnp.square(jnp.maximum(u, 0.0)).astype(jnp.bfloat16)
            part = jnp.dot(u, wdn_ref[c * FF_CHUNK:(c + 1) * FF_CHUNK, :], preferred_element_type=jnp.float32)
            acc = part if acc is None else acc + part
        o_ref[rows] = x1 + _rms(acc, g3_ref[...])


def _post(mix_d, mix_f, x2d, wod, wof, wup, wdn, g1, g2, g3):
    R, D = x2d.shape
    tm = TM_POST
    resident = lambda shape: pl.BlockSpec(shape, lambda r: (0,) * len(shape),
                                          pipeline_mode=pl.Buffered(1))
    return pl.pallas_call(
        _post_kernel,
        grid=(R // tm,),
        in_specs=[
            pl.BlockSpec((tm, DIFF_WIDTH), lambda r: (r, 0)),
            pl.BlockSpec((tm, FOX_WIDTH), lambda r: (r, 0)),
            pl.BlockSpec((tm, D), lambda r: (r, 0)),
            resident(wod.shape), resident(wof.shape), resident(wup.shape), resident(wdn.shape),
            resident((1, D)), resident((1, D)), resident((1, D)),
        ],
        out_specs=pl.BlockSpec((tm, D), lambda r: (r, 0)),
        out_shape=jax.ShapeDtypeStruct((R, D), jnp.float32),
        compiler_params=pltpu.CompilerParams(
            dimension_semantics=("parallel",), vmem_limit_bytes=VMEM_LIMIT),
        name="post",
    )(mix_d, mix_f, x2d, wod, wof, wup, wdn, g1, g2, g3)


def kernel(x, ln_attn_pre, w_in, b_f, lam_q1, lam_k1, lam_q2, lam_k2, subln_g, rel_bias,
           w_out, ln_attn_post, ln_mlp_pre, w_up, w_down, ln_mlp_post):
    B, S, D = x.shape
    depth = w_in.shape[0]
    bf16 = jnp.bfloat16
    bias_tiles, far_bucket = _bias_tiles(rel_bias, S)
    dw, fw = DIFF_WIDTH, FOX_WIDTH
    for l in range(depth):
        lambda_init = 0.8 - 0.6 * math.exp(-0.3 * l)
        w = w_in[l]
        o_dq, o_dk, o_dv, o_fq, o_fk, o_fv, o_g = 0, dw, 2 * dw, 3 * dw, 3 * dw + fw, 3 * dw + 2 * fw, 3 * dw + 3 * fw
        wrow = jnp.concatenate([w[:, o_dk:o_dk + dw], w[:, o_fk:o_fk + fw]], axis=1).astype(bf16)
        wcol = jnp.concatenate([w[:, o_dq:o_dq + dw], w[:, o_dv:o_dv + dw],
                                w[:, o_fq:o_fq + fw], w[:, o_fv:o_fv + fw]], axis=1).T.astype(bf16)
        wf = jnp.zeros((XROWS, D), bf16).at[:FOX_HEADS].set(w[:, o_g:o_g + FOX_HEADS].T.astype(bf16))
        bf_col = jnp.zeros((XROWS, 1), jnp.float32).at[:FOX_HEADS, 0].set(b_f[l].astype(jnp.float32))

        kd, kf, qd, vd, qf, vf, xq = _proj(x, ln_attn_pre[l][None, :], wrow, wcol, wf, bf_col)
        mix_d = _diff_attention(kd, qd, vd, bias_tiles, far_bucket, rel_bias,
                                lam_q1[l][None, :], lam_k1[l][None, :], lam_q2[l][None, :],
                                lam_k2[l][None, :], subln_g[l][None, :], lambda_init)
        mix_f = _fox_attention(kf, qf, xq, vf)
        wo = w_out[l].astype(bf16)
        y = _post(mix_d.reshape(B * S, dw), mix_f.reshape(B * S, fw), x.reshape(B * S, D),
                  wo[:dw], wo[dw:], w_up[l].astype(bf16), w_down[l].astype(bf16),
                  ln_attn_post[l][None, :], ln_mlp_pre[l][None, :], ln_mlp_post[l][None, :])
        x = y.reshape(B, S, D)
    return x
```

```python
import functools
import math

import numpy as np
import jax
import jax.numpy as jnp
from jax import lax
from jax.experimental import pallas as pl
from jax.experimental.pallas import tpu as pltpu

D_MODEL = 1024
DIFF_HEADS = 4
DIFF_HEAD_DIM = 64
DIFF_WIDTH = DIFF_HEADS * 2 * DIFF_HEAD_DIM
FOX_HEADS = 8
FOX_HEAD_DIM = 64
FOX_WIDTH = FOX_HEADS * FOX_HEAD_DIM
FOX_UNITS = FOX_HEADS // 2
D_FF = 4 * D_MODEL
N_BUCKETS = 32
MAX_DISTANCE = 128
EPS = 1e-6
NEG = -1e30
QK_SCALE = DIFF_HEAD_DIM ** -0.5
LOG2E = math.log2(math.e)

TQ = 256
TK = 256
TM_PROJ = 1024
TM_POST = 1024
POST_SUB = 512
FF_CHUNK = 1024
LOOKAHEAD = 4
ONES_ROWS = 16
DV_ROWS = 2 * DIFF_HEAD_DIM + ONES_ROWS
FV_ROWS = FOX_HEAD_DIM + ONES_ROWS
XROWS = 16
VMEM_LIMIT = 56 * 1024 * 1024


def _rms(x, g):
    ms = jnp.mean(x * x, axis=-1, keepdims=True)
    return x * lax.rsqrt(ms + EPS) * g


def _proj_kernel(x_ref, g_ref, wrow_ref, wcol_ref, wf_ref, bf_ref, pq_ref, pk_ref, oq_ref, ok_ref,
                 kd_ref, kf_ref, qd_ref, vd_ref, qf_ref, vf_ref, xq_ref, carry_ref):
    tm = x_ref.shape[1]
    nsub = tm // TQ
    nt = (((1,), (1,)), ((), ()))
    ones = jnp.ones((ONES_ROWS, TQ), jnp.bfloat16)
    outs = ((qd_ref, None, None), (vd_ref, 2 * DIFF_HEAD_DIM, DV_ROWS),
            (qf_ref, None, None), (vf_ref, FOX_HEAD_DIM, FV_ROWS))

    def column_outputs(c, h, n):
        o_ref, hrows, orows = outs[n]
        colt = lax.dot_general(wcol_ref[n * 512:(n + 1) * 512, :], h, nt,
                               preferred_element_type=jnp.float32)
        if hrows is None:
            o_ref[0, c] = (colt * (QK_SCALE * LOG2E)).astype(jnp.bfloat16)
        else:
            colt = colt.astype(jnp.bfloat16)
            for hd in range(512 // hrows):
                o_ref[0, c, orows * hd:orows * hd + hrows] = colt[hrows * hd:hrows * (hd + 1)]
                o_ref[0, c, orows * hd + hrows:orows * (hd + 1)] = ones

    ft_parts = []
    for c in range(nsub):
        rows = slice(c * TQ, (c + 1) * TQ)
        h = _rms(x_ref[0, rows], g_ref[...]).astype(jnp.bfloat16)
        ft_parts.append(lax.dot_general(wf_ref[...], h, nt, preferred_element_type=jnp.float32))
        krow = jnp.dot(h, wrow_ref[...], preferred_element_type=jnp.float32)
        kd_ref[0, rows] = krow[:, :DIFF_WIDTH].astype(jnp.bfloat16)
        for u in range(FOX_UNITS):
            kf_ref[0, rows, 256 * u:256 * u + 128] = (
                krow[:, DIFF_WIDTH + 128 * u:DIFF_WIDTH + 128 * (u + 1)].astype(jnp.bfloat16))
        for n in range(len(outs)):
            column_outputs(c, h, n)

    ft = jnp.concatenate(ft_parts, axis=1) + bf_ref[...]
    ls = jnp.minimum(ft, 0.0) - jnp.log1p(jnp.exp(-jnp.abs(ft)))
    lane = lax.broadcasted_iota(jnp.int32, ls.shape, 1)
    sh = 1
    while sh < tm:
        ls = ls + jnp.where(lane >= sh, pltpu.roll(ls, sh, axis=1), 0.0)
        sh *= 2

    @pl.when(pl.program_id(1) == 0)
    def _():
        carry_ref[...] = jnp.zeros_like(carry_ref)

    cum = ls + carry_ref[...]
    carry_ref[...] = cum[:, tm - 1:tm]
    cum = cum * LOG2E

    hi = cum.astype(jnp.bfloat16)
    r1 = cum - hi.astype(jnp.float32)
    mid = r1.astype(jnp.bfloat16)
    lo = (r1 - mid.astype(jnp.float32)).astype(jnp.bfloat16)
    parts = jnp.concatenate([hi, mid, lo, jnp.zeros((128 - 3 * XROWS, tm), jnp.bfloat16)], axis=0)
    xq = jnp.dot(pq_ref[...], parts, preferred_element_type=jnp.float32) + oq_ref[...]
    xq = xq.astype(jnp.bfloat16)
    for c in range(nsub):
        xq_ref[0, c] = xq[:, c * TQ:(c + 1) * TQ]
    xk = jnp.dot(pk_ref[...], parts, preferred_element_type=jnp.float32) + ok_ref[...]
    for u in range(FOX_UNITS):
        kf_ref[0, :, 256 * u + 128:256 * (u + 1)] = (
            xk[128 * u:128 * (u + 1), :].T.astype(jnp.bfloat16))


def _placement_constants():
    pq = np.zeros((FOX_UNITS * 2 * XROWS, 128), np.float32)
    oq = np.zeros((FOX_UNITS * 2 * XROWS, 1), np.float32)
    pk = np.zeros((FOX_UNITS * 128, 128), np.float32)
    ok = np.zeros((FOX_UNITS * 128, 1), np.float32)
    for u in range(FOX_UNITS):
        for a in range(2):
            head = 2 * u + a
            for r in range(3):
                pq[2 * XROWS * u + XROWS * a + r, XROWS * r + head] = 1.0
                oq[2 * XROWS * u + XROWS * a + 3 + r, 0] = 1.0
                pk[128 * u + XROWS * a + 3 + r, XROWS * r + head] = -1.0
                ok[128 * u + XROWS * a + r, 0] = 1.0
    return (jnp.asarray(pq, jnp.bfloat16), jnp.asarray(pk, jnp.bfloat16),
            jnp.asarray(oq), jnp.asarray(ok))


def _proj(x, g, wrow, wcol, wf, bf_col):
    B, S, D = x.shape
    tm = TM_PROJ
    nsub = tm // TQ
    nq = S // TQ
    pq, pk, oq, ok = _placement_constants()
    const = lambda shape: pl.BlockSpec(shape, lambda b, t: (0,) * len(shape))
    colspec = lambda rows: pl.BlockSpec((1, nsub, rows, TQ), lambda b, t: (b, t, 0, 0))
    bf16 = jnp.bfloat16
    return pl.pallas_call(
        _proj_kernel,
        grid=(B, S // tm),
        in_specs=[
            pl.BlockSpec((1, tm, D), lambda b, t: (b, t, 0)),
            const((1, D)), const(wrow.shape), const(wcol.shape), const(wf.shape), const(bf_col.shape),
            const(pq.shape), const(pk.shape), const(oq.shape), const(ok.shape),
        ],
        out_specs=[
            pl.BlockSpec((1, tm, DIFF_WIDTH), lambda b, t: (b, t, 0)),
            pl.BlockSpec((1, tm, 2 * FOX_WIDTH), lambda b, t: (b, t, 0)),
            colspec(512), colspec(DIFF_HEADS * DV_ROWS), colspec(512), colspec(FOX_HEADS * FV_ROWS),
            colspec(FOX_UNITS * 2 * XROWS),
        ],
        out_shape=[
            jax.ShapeDtypeStruct((B, S, DIFF_WIDTH), bf16),
            jax.ShapeDtypeStruct((B, S, 2 * FOX_WIDTH), bf16),
            jax.ShapeDtypeStruct((B, nq, 512, TQ), bf16),
            jax.ShapeDtypeStruct((B, nq, DIFF_HEADS * DV_ROWS, TQ), bf16),
            jax.ShapeDtypeStruct((B, nq, 512, TQ), bf16),
            jax.ShapeDtypeStruct((B, nq, FOX_HEADS * FV_ROWS, TQ), bf16),
            jax.ShapeDtypeStruct((B, nq, FOX_UNITS * 2 * XROWS, TQ), bf16),
        ],
        scratch_shapes=[pltpu.VMEM((XROWS, 1), jnp.float32)],
        compiler_params=pltpu.CompilerParams(
            dimension_semantics=("arbitrary", "arbitrary"), vmem_limit_bytes=VMEM_LIMIT),
        name="proj",
    )(x, g, wrow, wcol, wf, bf_col, pq, pk, oq, ok)


def _t5_buckets(n):
    max_exact = N_BUCKETS // 2
    dist = np.arange(n)
    d = np.maximum(dist, 1).astype(np.float32)
    large = np.float32(max_exact) + (np.log(d / np.float32(max_exact))
                                     / np.float32(math.log(MAX_DISTANCE / max_exact))
                                     * np.float32(N_BUCKETS - max_exact))
    large = np.minimum(large.astype(np.int32), N_BUCKETS - 1)
    return np.where(dist < max_exact, dist, large).astype(np.int32)


def _bias_kernel(rel_ref, idx_ref, out_ref):
    hd = pl.program_id(0)
    idx = idx_ref[...]
    val = jnp.full(idx.shape, NEG, jnp.float32)
    for bkt in range(N_BUCKETS):
        val = jnp.where(idx == bkt, rel_ref[bkt, hd] * LOG2E, val)
    out_ref[0] = val


def _bias_tiles(rel_bias, S):
    buckets = _t5_buckets(S)
    far = int(buckets[-1])
    assert np.all(buckets[TQ + 1:] == far)
    s_loc = np.arange(TK)[:, None]
    t_loc = np.arange(TQ)[None, :]
    rel0 = t_loc - s_loc
    idx0 = np.where(rel0 >= 0, buckets[np.maximum(rel0, 0)], -1)
    idx1 = buckets[TQ + t_loc - s_loc]
    idx = jnp.asarray(np.concatenate([idx1, idx0], axis=0).astype(np.int32))
    tiles = pl.pallas_call(
        _bias_kernel,
        grid=(DIFF_HEADS,),
        in_specs=[pl.BlockSpec(memory_space=pltpu.SMEM),
                  pl.BlockSpec((2 * TK, TQ), lambda h: (0, 0))],
        out_specs=pl.BlockSpec((1, 2 * TK, TQ), lambda h: (h, 0, 0)),
        out_shape=jax.ShapeDtypeStruct((DIFF_HEADS, 2 * TK, TQ), jnp.float32),
        name="t5_bias_tiles",
    )(rel_bias, idx)
    return tiles, far


def _flash_update(m_ref, acc_ref, a, st, vt, shift=None):
    m_old = m_ref[a]
    mx = jnp.max(st, axis=0, keepdims=True)
    if shift is None:
        m_new = jnp.maximum(m_old, mx)
        x = st - m_new
    else:
        m_new = jnp.maximum(m_old, mx + shift)
        x = st - (m_new - shift)
    p = jnp.exp2(x.astype(jnp.bfloat16))
    alpha = jnp.exp2(m_old - m_new)
    acc_ref[a] = alpha * acc_ref[a] + jnp.dot(vt, p, preferred_element_type=jnp.float32)
    m_ref[a] = m_new


def _flash_init(m_ref, acc_ref):
    m_ref[...] = jnp.full(m_ref.shape, NEG, jnp.float32)
    acc_ref[...] = jnp.zeros_like(acc_ref)


def _run_chains(n, st_ref, score_fn, update_fn, j, rows, prefetch_fn):
    pending = [(c, st_ref[c, :rows]) for c in range(LOOKAHEAD)]
    for c in range(LOOKAHEAD, n):
        pending.append((c, score_fn(c, j, rows)))
        update_fn(*pending.pop(0))
    for c in range(LOOKAHEAD):
        st_ref[c] = prefetch_fn(c)
        update_fn(*pending.pop(0))


def _value_rows(v_ref, j, rows, r0, nrows):
    blocks = [v_ref[0, j + b, r0:r0 + nrows] for b in range(rows // TK)]
    return blocks[0] if len(blocks) == 1 else jnp.concatenate(blocks, axis=1)


def _stage_keys(k_ref, kbuf_ref, qpad_ref, first):
    @pl.when(first)
    def _():
        def body(t, carry):
            r = pl.multiple_of(t * TK, TK)
            kbuf_ref[pl.ds(r, TK), :] = k_ref[0, pl.ds(r, TK), :]
            return carry

        lax.fori_loop(0, k_ref.shape[1] // TK, body, 0)
        qpad_ref[...] = jnp.zeros_like(qpad_ref)


def _first_step_prime(first, st_ref, fill_queries, q_ref, score_fn):
    @pl.when(first)
    def _():
        fill_queries(0, q_ref)
        for c in range(LOOKAHEAD):
            st_ref[c] = score_fn(c, 0, 2 * TK, 0)


def _key_block_schedule(i, step):
    nfar = jnp.maximum(i - 1, 0)

    def far_pair(jj, carry):
        step(2 * jj, 2 * TK, "far", 2 * jj + 2)
        return carry

    lax.fori_loop(0, lax.shift_right_logical(nfar, 1), far_pair, 0)

    @pl.when((nfar & 1) == 1)
    def _():
        step(i - 2, TK, "far", i - 1)

    @pl.when(i >= 1)
    def _():
        step(i - 1, 2 * TK, "last", None)

    @pl.when(i == 0)
    def _():
        step(0, TK, "diag_only", None)


def _diff_kernel(far_bucket, lambda_init,
                 rel_ref, k_ref, q_ref, qnext_ref, v_ref, bias_ref, lq1_ref, lk1_ref, lq2_ref, lk2_ref,
                 g_ref, o_ref, qpad_ref, m_ref, acc_ref, st_ref, kbuf_ref):
    i = pl.program_id(1)
    d = DIFF_HEAD_DIM
    cur, nxt = i & 1, (i + 1) & 1
    _stage_keys(k_ref, kbuf_ref, qpad_ref, i == 0)

    def fill_queries(slot, src_ref):
        for c in range(2 * DIFF_HEADS):
            r0 = d * (c % 4)
            qpad_ref[slot, c, r0:r0 + d] = src_ref[0, 0, d * c:d * (c + 1)]

    def score(c, j, rows, slot=cur):
        pair = c // 4
        kblk = kbuf_ref[pl.ds(pl.multiple_of(j * TK, TK), rows), 256 * pair:256 * (pair + 1)]
        return jnp.dot(kblk, qpad_ref[slot, c], preferred_element_type=jnp.float32)

    _first_step_prime(i == 0, st_ref, fill_queries, q_ref, score)
    fill_queries(nxt, qnext_ref)
    _flash_init(m_ref, acc_ref)

    def step(j, rows, kind, prefetch_j):
        if prefetch_j is None:
            prefetch = lambda c: score(c, 0, 2 * TK, nxt)
        else:
            prefetch = lambda c: score(c, prefetch_j, 2 * TK)

        def update(c, st):
            hd = c // 2
            vt = _value_rows(v_ref, j, rows, DV_ROWS * hd, DV_ROWS)
            if kind == "far":
                _flash_update(m_ref, acc_ref, c, st, vt, shift=rel_ref[far_bucket, hd] * LOG2E)
            else:
                bias = bias_ref[hd] if kind == "last" else bias_ref[hd, TK:]
                _flash_update(m_ref, acc_ref, c, bias + st, vt)

        _run_chains(2 * DIFF_HEADS, st_ref, score, update, j, rows, prefetch)

    _key_block_schedule(i, step)

    lam = (jnp.exp(jnp.sum(lq1_ref[...] * lk1_ref[...], axis=-1, keepdims=True))
           - jnp.exp(jnp.sum(lq2_ref[...] * lk2_ref[...], axis=-1, keepdims=True))
           + lambda_init)
    for hd in range(DIFF_HEADS):
        dv = 2 * d
        o0 = acc_ref[2 * hd, :dv] * (1.0 / acc_ref[2 * hd, dv:dv + 1])
        o1 = acc_ref[2 * hd + 1, :dv] * (1.0 / acc_ref[2 * hd + 1, dv:dv + 1])
        od = o0 - lam * o1
        ms = jnp.mean(od * od, axis=0, keepdims=True)
        y = (od * lax.rsqrt(ms + EPS)).T * g_ref[...]
        o_ref[0, :, 128 * hd:128 * (hd + 1)] = (y * (1.0 - lambda_init)).astype(o_ref.dtype)


def _diff_attention(kd, qd, vd, bias_tiles, far_bucket, rel_bias, lq1, lk1, lq2, lk2, g, lambda_init):
    B, S, _ = kd.shape
    nq = S // TQ
    nmaps = 2 * DIFF_HEADS
    vec = lambda n: pl.BlockSpec((1, n), lambda b, i: (0, 0))
    return pl.pallas_call(
        functools.partial(_diff_kernel, far_bucket, lambda_init),
        grid=(B, nq),
        in_specs=[
            pl.BlockSpec(memory_space=pltpu.SMEM),
            pl.BlockSpec((1, S, DIFF_WIDTH), lambda b, i: (b, 0, 0)),
            pl.BlockSpec((1, 1, DIFF_WIDTH, TQ), lambda b, i: (b, i, 0, 0)),
            pl.BlockSpec((1, 1, DIFF_WIDTH, TQ), lambda b, i: (b, jnp.minimum(i + 1, nq - 1), 0, 0)),
            pl.BlockSpec((1, nq, DIFF_HEADS * DV_ROWS, TK), lambda b, i: (b, 0, 0, 0)),
            pl.BlockSpec((DIFF_HEADS, 2 * TK, TQ), lambda b, i: (0, 0, 0)),
            vec(DIFF_HEAD_DIM), vec(DIFF_HEAD_DIM), vec(DIFF_HEAD_DIM), vec(DIFF_HEAD_DIM),
            vec(2 * DIFF_HEAD_DIM),
        ],
        out_specs=pl.BlockSpec((1, TQ, DIFF_WIDTH), lambda b, i: (b, i, 0)),
        out_shape=jax.ShapeDtypeStruct((B, S, DIFF_WIDTH), jnp.bfloat16),
        scratch_shapes=[
            pltpu.VMEM((2, nmaps, 256, TQ), jnp.bfloat16),
            pltpu.VMEM((nmaps, 1, TQ), jnp.float32),
            pltpu.VMEM((nmaps, DV_ROWS, TQ), jnp.float32),
            pltpu.VMEM((LOOKAHEAD, 2 * TK, TQ), jnp.float32),
            pltpu.VMEM((S, DIFF_WIDTH), jnp.bfloat16),
        ],
        compiler_params=pltpu.CompilerParams(
            dimension_semantics=("arbitrary", "arbitrary"), vmem_limit_bytes=VMEM_LIMIT),
        name="diff_attention",
    )(rel_bias, kd, qd, qd, vd, bias_tiles, lq1, lk1, lq2, lk2, g)


def _fox_kernel(k_ref, q_ref, xq_ref, qnext_ref, xqnext_ref, v_ref, o_ref,
                qpad_ref, m_ref, acc_ref, st_ref, kbuf_ref):
    i = pl.program_id(1)
    d = FOX_HEAD_DIM
    cur, nxt = i & 1, (i + 1) & 1
    _stage_keys(k_ref, kbuf_ref, qpad_ref, i == 0)

    def fill_queries(slot, src_refs):
        qsrc_ref, xsrc_ref = src_refs
        for hd in range(FOX_HEADS):
            a = hd % 2
            qpad_ref[slot, hd, d * a:d * (a + 1)] = qsrc_ref[0, 0, d * hd:d * (hd + 1)]
            qpad_ref[slot, hd, 128 + XROWS * a:128 + XROWS * (a + 1)] = (
                xsrc_ref[0, 0, XROWS * hd:XROWS * (hd + 1)])

    def score(hd, j, rows, slot=cur):
        u = hd // 2
        kblk = kbuf_ref[pl.ds(pl.multiple_of(j * TK, TK), rows), 256 * u:256 * (u + 1)]
        return jnp.dot(kblk, qpad_ref[slot, hd], preferred_element_type=jnp.float32)

    _first_step_prime(i == 0, st_ref, fill_queries, (q_ref, xq_ref), score)
    fill_queries(nxt, (qnext_ref, xqnext_ref))
    _flash_init(m_ref, acc_ref)

    def step(j, rows, kind, prefetch_j):
        if prefetch_j is None:
            prefetch = lambda hd: score(hd, 0, 2 * TK, nxt)
        else:
            prefetch = lambda hd: score(hd, prefetch_j, 2 * TK)

        def update(hd, st):
            if kind != "far":
                row = lax.broadcasted_iota(jnp.int32, st.shape, 0) - (rows - TK)
                col = lax.broadcasted_iota(jnp.int32, st.shape, 1)
                st = jnp.where(row > col, NEG, st)
            _flash_update(m_ref, acc_ref, hd, st, _value_rows(v_ref, j, rows, FV_ROWS * hd, FV_ROWS))

        _run_chains(FOX_HEADS, st_ref, score, update, j, rows, prefetch)

    _key_block_schedule(i, step)

    for u in range(FOX_UNITS):
        o = jnp.concatenate([acc_ref[hd, :d] * (1.0 / acc_ref[hd, d:d + 1]) for hd in (2 * u, 2 * u + 1)],
                            axis=0)
        o_ref[0, :, 128 * u:128 * (u + 1)] = o.T.astype(o_ref.dtype)


def _fox_attention(kf, qf, xq, vf):
    B, S, _ = kf.shape
    nq = S // TQ
    return pl.pallas_call(
        _fox_kernel,
        grid=(B, nq),
        in_specs=[
            pl.BlockSpec((1, S, 2 * FOX_WIDTH), lambda b, i: (b, 0, 0)),
            pl.BlockSpec((1, 1, FOX_WIDTH, TQ), lambda b, i: (b, i, 0, 0)),
            pl.BlockSpec((1, 1, FOX_HEADS * XROWS, TQ), lambda b, i: (b, i, 0, 0)),
            pl.BlockSpec((1, 1, FOX_WIDTH, TQ), lambda b, i: (b, jnp.minimum(i + 1, nq - 1), 0, 0)),
            pl.BlockSpec((1, 1, FOX_HEADS * XROWS, TQ), lambda b, i: (b, jnp.minimum(i + 1, nq - 1), 0, 0)),
            pl.BlockSpec((1, nq, FOX_HEADS * FV_ROWS, TK), lambda b, i: (b, 0, 0, 0)),
        ],
        out_specs=pl.BlockSpec((1, TQ, FOX_WIDTH), lambda b, i: (b, i, 0)),
        out_shape=jax.ShapeDtypeStruct((B, S, FOX_WIDTH), jnp.bfloat16),
        scratch_shapes=[
            pltpu.VMEM((2, FOX_HEADS, 256, TQ), jnp.bfloat16),
            pltpu.VMEM((FOX_HEADS, 1, TQ), jnp.float32),
            pltpu.VMEM((FOX_HEADS, FV_ROWS, TQ), jnp.float32),
            pltpu.VMEM((LOOKAHEAD, 2 * TK, TQ), jnp.float32),
            pltpu.VMEM((S, 2 * FOX_WIDTH), jnp.bfloat16),
        ],
        compiler_params=pltpu.CompilerParams(
            dimension_semantics=("arbitrary", "arbitrary"), vmem_limit_bytes=VMEM_LIMIT),
        name="fox_attention",
    )(kf, qf, xq, qf, xq, vf)


def _post_kernel(md_ref, mf_ref, x_ref, wod_ref, wof_ref, wup_ref, wdn_ref, g1_ref, g2_ref, g3_ref,
                 o_ref):
    subs = [slice(s * POST_SUB, (s + 1) * POST_SUB) for s in range(x_ref.shape[0] // POST_SUB)]
    attn = [jnp.dot(md_ref[rows], wod_ref[...], preferred_element_type=jnp.float32)
            + jnp.dot(mf_ref[rows], wof_ref[...], preferred_element_type=jnp.float32) for rows in subs]
    for rows, a in zip(subs, attn):
        x1 = x_ref[rows] + _rms(a, g1_ref[...])
        h2 = _rms(x1, g2_ref[...]).astype(jnp.bfloat16)
        acc = None
        for c in range(D_FF // FF_CHUNK):
            u = jnp.dot(h2, wup_ref[:, c * FF_CHUNK:(c + 1) * FF_CHUNK], preferred_element_type=jnp.float32)
            u = jnp.square(jnp.maximum(u, 0.0)).astype(jnp.bfloat16)
            part = jnp.dot(u, wdn_ref[c * FF_CHUNK:(c + 1) * FF_CHUNK, :], preferred_element_type=jnp.float32)
            acc = part if acc is None else acc + part
        o_ref[rows] = x1 + _rms(acc, g3_ref[...])


def _post(mix_d, mix_f, x2d, wod, wof, wup, wdn, g1, g2, g3):
    R, D = x2d.shape
    tm = TM_POST
    resident = lambda shape: pl.BlockSpec(shape, lambda r: (0,) * len(shape),
                                          pipeline_mode=pl.Buffered(1))
    return pl.pallas_call(
        _post_kernel,
        grid=(R // tm,),
        in_specs=[
            pl.BlockSpec((tm, DIFF_WIDTH), lambda r: (r, 0)),
            pl.BlockSpec((tm, FOX_WIDTH), lambda r: (r, 0)),
            pl.BlockSpec((tm, D), lambda r: (r, 0)),
            resident(wod.shape), resident(wof.shape), resident(wup.shape), resident(wdn.shape),
            resident((1, D)), resident((1, D)), resident((1, D)),
        ],
        out_specs=pl.BlockSpec((tm, D), lambda r: (r, 0)),
        out_shape=jax.ShapeDtypeStruct((R, D), jnp.float32),
        compiler_params=pltpu.CompilerParams(
            dimension_semantics=("parallel",), vmem_limit_bytes=VMEM_LIMIT),
        name="post",
    )(mix_d, mix_f, x2d, wod, wof, wup, wdn, g1, g2, g3)


def kernel(x, ln_attn_pre, w_in, b_f, lam_q1, lam_k1, lam_q2, lam_k2, subln_g, rel_bias,
           w_out, ln_attn_post, ln_mlp_pre, w_up, w_down, ln_mlp_post):
    B, S, D = x.shape
    depth = w_in.shape[0]
    bf16 = jnp.bfloat16
    bias_tiles, far_bucket = _bias_tiles(rel_bias, S)
    dw, fw = DIFF_WIDTH, FOX_WIDTH
    for l in range(depth):
        lambda_init = 0.8 - 0.6 * math.exp(-0.3 * l)
        w = w_in[l]
        o_dq, o_dk, o_dv, o_fq, o_fk, o_fv, o_g = 0, dw, 2 * dw, 3 * dw, 3 * dw + fw, 3 * dw + 2 * fw, 3 * dw + 3 * fw
        wrow = jnp.concatenate([w[:, o_dk:o_dk + dw], w[:, o_fk:o_fk + fw]], axis=1).astype(bf16)
        wcol = jnp.concatenate([w[:, o_dq:o_dq + dw], w[:, o_dv:o_dv + dw],
                                w[:, o_fq:o_fq + fw], w[:, o_fv:o_fv + fw]], axis=1).T.astype(bf16)
        wf = jnp.zeros((XROWS, D), bf16).at[:FOX_HEADS].set(w[:, o_g:o_g + FOX_HEADS].T.astype(bf16))
        bf_col = jnp.zeros((XROWS, 1), jnp.float32).at[:FOX_HEADS, 0].set(b_f[l].astype(jnp.float32))

        kd, kf, qd, vd, qf, vf, xq = _proj(x, ln_attn_pre[l][None, :], wrow, wcol, wf, bf_col)
        mix_d = _diff_attention(kd, qd, vd, bias_tiles, far_bucket, rel_bias,
                                lam_q1[l][None, :], lam_k1[l][None, :], lam_q2[l][None, :],
                                lam_k2[l][None, :], subln_g[l][None, :], lambda_init)
        mix_f = _fox_attention(kf, qf, xq, vf)
        wo = w_out[l].astype(bf16)
        y = _post(mix_d.reshape(B * S, dw), mix_f.reshape(B * S, fw), x.reshape(B * S, D),
                  wo[:dw], wo[dw:], w_up[l].astype(bf16), w_down[l].astype(bf16),
                  ln_attn_post[l][None, :], ln_mlp_pre[l][None, :], ln_mlp_post[l][None, :])
        x = y.reshape(B, S, D)
    return x
```

```python
import functools
import math

import numpy as np
import jax
import jax.numpy as jnp
from jax import lax
from jax.experimental import pallas as pl
from jax.experimental.pallas import tpu as pltpu

D_MODEL = 1024
DIFF_HEADS = 4
DIFF_HEAD_DIM = 64
DIFF_WIDTH = DIFF_HEADS * 2 * DIFF_HEAD_DIM
FOX_HEADS = 8
FOX_HEAD_DIM = 64
FOX_WIDTH = FOX_HEADS * FOX_HEAD_DIM
FOX_UNITS = FOX_HEADS // 2
D_FF = 4 * D_MODEL
N_BUCKETS = 32
MAX_DISTANCE = 128
EPS = 1e-6
NEG = -1e30
QK_SCALE = DIFF_HEAD_DIM ** -0.5
LOG2E = math.log2(math.e)

TQ = 256
TK = 256
TM_PROJ = 1024
TM_POST = 1024
POST_SUB = 512
FF_CHUNK = 1024
LOOKAHEAD = 6
ONES_ROWS = 16
DV_ROWS = 2 * DIFF_HEAD_DIM + ONES_ROWS
FV_ROWS = FOX_HEAD_DIM + ONES_ROWS
XROWS = 16
VMEM_LIMIT = 56 * 1024 * 1024


def _rms(x, g):
    ms = jnp.mean(x * x, axis=-1, keepdims=True)
    return x * lax.rsqrt(ms + EPS) * g


def _proj_kernel(x_ref, g_ref, wrow_ref, wcol_ref, wf_ref, bf_ref, pq_ref, pk_ref, oq_ref, ok_ref,
                 kd_ref, kf_ref, qd_ref, vd_ref, qf_ref, vf_ref, xq_ref, carry_ref):
    tm = x_ref.shape[1]
    nsub = tm // TQ
    nt = (((1,), (1,)), ((), ()))
    ones = jnp.ones((ONES_ROWS, TQ), jnp.bfloat16)
    outs = ((qd_ref, None, None), (vd_ref, 2 * DIFF_HEAD_DIM, DV_ROWS),
            (qf_ref, None, None), (vf_ref, FOX_HEAD_DIM, FV_ROWS))

    def column_outputs(c, h, n):
        o_ref, hrows, orows = outs[n]
        colt = lax.dot_general(wcol_ref[n * 512:(n + 1) * 512, :], h, nt,
                               preferred_element_type=jnp.float32)
        if hrows is None:
            o_ref[0, c] = (colt * (QK_SCALE * LOG2E)).astype(jnp.bfloat16)
        else:
            colt = colt.astype(jnp.bfloat16)
            for hd in range(512 // hrows):
                o_ref[0, c, orows * hd:orows * hd + hrows] = colt[hrows * hd:hrows * (hd + 1)]
                o_ref[0, c, orows * hd + hrows:orows * (hd + 1)] = ones

    ft_parts = []
    for c in range(nsub):
        rows = slice(c * TQ, (c + 1) * TQ)
        h = _rms(x_ref[0, rows], g_ref[...]).astype(jnp.bfloat16)
        ft_parts.append(lax.dot_general(wf_ref[...], h, nt, preferred_element_type=jnp.float32))
        krow = jnp.dot(h, wrow_ref[...], preferred_element_type=jnp.float32)
        kd_ref[0, rows] = krow[:, :DIFF_WIDTH].astype(jnp.bfloat16)
        for u in range(FOX_UNITS):
            kf_ref[0, rows, 256 * u:256 * u + 128] = (
                krow[:, DIFF_WIDTH + 128 * u:DIFF_WIDTH + 128 * (u + 1)].astype(jnp.bfloat16))
        for n in range(len(outs)):
            column_outputs(c, h, n)

    ft = jnp.concatenate(ft_parts, axis=1) + bf_ref[...]
    ls = jnp.minimum(ft, 0.0) - jnp.log1p(jnp.exp(-jnp.abs(ft)))
    lane = lax.broadcasted_iota(jnp.int32, ls.shape, 1)
    sh = 1
    while sh < tm:
        ls = ls + jnp.where(lane >= sh, pltpu.roll(ls, sh, axis=1), 0.0)
        sh *= 2

    @pl.when(pl.program_id(1) == 0)
    def _():
        carry_ref[...] = jnp.zeros_like(carry_ref)

    cum = ls + carry_ref[...]
    carry_ref[...] = cum[:, tm - 1:tm]
    cum = cum * LOG2E

    hi = cum.astype(jnp.bfloat16)
    r1 = cum - hi.astype(jnp.float32)
    mid = r1.astype(jnp.bfloat16)
    lo = (r1 - mid.astype(jnp.float32)).astype(jnp.bfloat16)
    parts = jnp.concatenate([hi, mid, lo, jnp.zeros((128 - 3 * XROWS, tm), jnp.bfloat16)], axis=0)
    xq = jnp.dot(pq_ref[...], parts, preferred_element_type=jnp.float32) + oq_ref[...]
    xq = xq.astype(jnp.bfloat16)
    for c in range(nsub):
        xq_ref[0, c] = xq[:, c * TQ:(c + 1) * TQ]
    xk = jnp.dot(pk_ref[...], parts, preferred_element_type=jnp.float32) + ok_ref[...]
    for u in range(FOX_UNITS):
        kf_ref[0, :, 256 * u + 128:256 * (u + 1)] = (
            xk[128 * u:128 * (u + 1), :].T.astype(jnp.bfloat16))


def _placement_constants():
    pq = np.zeros((FOX_UNITS * 2 * XROWS, 128), np.float32)
    oq = np.zeros((FOX_UNITS * 2 * XROWS, 1), np.float32)
    pk = np.zeros((FOX_UNITS * 128, 128), np.float32)
    ok = np.zeros((FOX_UNITS * 128, 1), np.float32)
    for u in range(FOX_UNITS):
        for a in range(2):
            head = 2 * u + a
            for r in range(3):
                pq[2 * XROWS * u + XROWS * a + r, XROWS * r + head] = 1.0
                oq[2 * XROWS * u + XROWS * a + 3 + r, 0] = 1.0
                pk[128 * u + XROWS * a + 3 + r, XROWS * r + head] = -1.0
                ok[128 * u + XROWS * a + r, 0] = 1.0
    return (jnp.asarray(pq, jnp.bfloat16), jnp.asarray(pk, jnp.bfloat16),
            jnp.asarray(oq), jnp.asarray(ok))


def _proj(x, g, wrow, wcol, wf, bf_col):
    B, S, D = x.shape
    tm = TM_PROJ
    nsub = tm // TQ
    nq = S // TQ
    pq, pk, oq, ok = _placement_constants()
    const = lambda shape: pl.BlockSpec(shape, lambda b, t: (0,) * len(shape))
    colspec = lambda rows: pl.BlockSpec((1, nsub, rows, TQ), lambda b, t: (b, t, 0, 0))
    bf16 = jnp.bfloat16
    return pl.pallas_call(
        _proj_kernel,
        grid=(B, S // tm),
        in_specs=[
            pl.BlockSpec((1, tm, D), lambda b, t: (b, t, 0)),
            const((1, D)), const(wrow.shape), const(wcol.shape), const(wf.shape), const(bf_col.shape),
            const(pq.shape), const(pk.shape), const(oq.shape), const(ok.shape),
        ],
        out_specs=[
            pl.BlockSpec((1, tm, DIFF_WIDTH), lambda b, t: (b, t, 0)),
            pl.BlockSpec((1, tm, 2 * FOX_WIDTH), lambda b, t: (b, t, 0)),
            colspec(512), colspec(DIFF_HEADS * DV_ROWS), colspec(512), colspec(FOX_HEADS * FV_ROWS),
            colspec(FOX_UNITS * 2 * XROWS),
        ],
        out_shape=[
            jax.ShapeDtypeStruct((B, S, DIFF_WIDTH), bf16),
            jax.ShapeDtypeStruct((B, S, 2 * FOX_WIDTH), bf16),
            jax.ShapeDtypeStruct((B, nq, 512, TQ), bf16),
            jax.ShapeDtypeStruct((B, nq, DIFF_HEADS * DV_ROWS, TQ), bf16),
            jax.ShapeDtypeStruct((B, nq, 512, TQ), bf16),
            jax.ShapeDtypeStruct((B, nq, FOX_HEADS * FV_ROWS, TQ), bf16),
            jax.ShapeDtypeStruct((B, nq, FOX_UNITS * 2 * XROWS, TQ), bf16),
        ],
        scratch_shapes=[pltpu.VMEM((XROWS, 1), jnp.float32)],
        compiler_params=pltpu.CompilerParams(
            dimension_semantics=("arbitrary", "arbitrary"), vmem_limit_bytes=VMEM_LIMIT),
        name="proj",
    )(x, g, wrow, wcol, wf, bf_col, pq, pk, oq, ok)


def _t5_buckets(n):
    max_exact = N_BUCKETS // 2
    dist = np.arange(n)
    d = np.maximum(dist, 1).astype(np.float32)
    large = np.float32(max_exact) + (np.log(d / np.float32(max_exact))
                                     / np.float32(math.log(MAX_DISTANCE / max_exact))
                                     * np.float32(N_BUCKETS - max_exact))
    large = np.minimum(large.astype(np.int32), N_BUCKETS - 1)
    return np.where(dist < max_exact, dist, large).astype(np.int32)


def _bias_kernel(rel_ref, idx_ref, out_ref):
    hd = pl.program_id(0)
    idx = idx_ref[...]
    val = jnp.full(idx.shape, NEG, jnp.float32)
    for bkt in range(N_BUCKETS):
        val = jnp.where(idx == bkt, rel_ref[bkt, hd] * LOG2E, val)
    out_ref[0] = val


def _bias_tiles(rel_bias, S):
    buckets = _t5_buckets(S)
    far = int(buckets[-1])
    assert np.all(buckets[TQ + 1:] == far)
    s_loc = np.arange(TK)[:, None]
    t_loc = np.arange(TQ)[None, :]
    rel0 = t_loc - s_loc
    idx0 = np.where(rel0 >= 0, buckets[np.maximum(rel0, 0)], -1)
    idx1 = buckets[TQ + t_loc - s_loc]
    idx = jnp.asarray(np.concatenate([idx1, idx0], axis=0).astype(np.int32))
    tiles = pl.pallas_call(
        _bias_kernel,
        grid=(DIFF_HEADS,),
        in_specs=[pl.BlockSpec(memory_space=pltpu.SMEM),
                  pl.BlockSpec((2 * TK, TQ), lambda h: (0, 0))],
        out_specs=pl.BlockSpec((1, 2 * TK, TQ), lambda h: (h, 0, 0)),
        out_shape=jax.ShapeDtypeStruct((DIFF_HEADS, 2 * TK, TQ), jnp.float32),
        name="t5_bias_tiles",
    )(rel_bias, idx)
    return tiles, far


def _flash_update(m_ref, acc_ref, a, st, vt, shift=None):
    m_old = m_ref[a]
    mx = jnp.max(st, axis=0, keepdims=True)
    if shift is None:
        m_new = jnp.maximum(m_old, mx)
        x = st - m_new
    else:
        m_new = jnp.maximum(m_old, mx + shift)
        x = st - (m_new - shift)
    p = jnp.exp2(x.astype(jnp.bfloat16))
    alpha = jnp.exp2(m_old - m_new)
    acc_ref[a] = alpha * acc_ref[a] + jnp.dot(vt, p, preferred_element_type=jnp.float32)
    m_ref[a] = m_new


def _flash_init(m_ref, acc_ref):
    m_ref[...] = jnp.full(m_ref.shape, NEG, jnp.float32)
    acc_ref[...] = jnp.zeros_like(acc_ref)


def _run_chains(n, st_ref, score_fn, update_fn, j, rows, prefetch_fn):
    def staged(c):
        if rows <= 2 * TK:
            return st_ref[c, :rows]
        return jnp.concatenate([st_ref[c], score_fn(c, j + 2, rows - 2 * TK)], axis=0)

    pending = [(c, staged(c)) for c in range(LOOKAHEAD)]
    for c in range(LOOKAHEAD, n):
        pending.append((c, score_fn(c, j, rows)))
        update_fn(*pending.pop(0))
    for c in range(LOOKAHEAD):
        st_ref[c] = prefetch_fn(c)
        update_fn(*pending.pop(0))


def _value_rows(v_ref, j, rows, r0, nrows):
    blocks = [v_ref[0, j + b, r0:r0 + nrows] for b in range(rows // TK)]
    return blocks[0] if len(blocks) == 1 else jnp.concatenate(blocks, axis=1)


def _stage_keys(k_ref, kbuf_ref, qpad_ref, first):
    @pl.when(first)
    def _():
        def body(t, carry):
            r = pl.multiple_of(t * TK, TK)
            kbuf_ref[pl.ds(r, TK), :] = k_ref[0, pl.ds(r, TK), :]
            return carry

        lax.fori_loop(0, k_ref.shape[1] // TK, body, 0)
        qpad_ref[...] = jnp.zeros_like(qpad_ref)


def _first_step_prime(first, st_ref, fill_queries, q_ref, score_fn):
    @pl.when(first)
    def _():
        fill_queries(0, q_ref)
        for c in range(LOOKAHEAD):
            st_ref[c] = score_fn(c, 0, 2 * TK, 0)


def _key_block_schedule(i, step):
    nfar = jnp.maximum(i - 1, 0)
    odd = (nfar & 1) == 1

    def far_pair(jj, carry):
        step(2 * jj, 2 * TK, "far", 2 * jj + 2)
        return carry

    lax.fori_loop(0, lax.shift_right_logical(nfar, 1), far_pair, 0)

    @pl.when(odd)
    def _():
        step(i - 2, 3 * TK, "last", None)

    @pl.when(jnp.logical_and(i >= 1, jnp.logical_not(odd)))
    def _():
        step(i - 1, 2 * TK, "last", None)

    @pl.when(i == 0)
    def _():
        step(0, TK, "last", None)


def _diff_kernel(far_bucket, lambda_init,
                 rel_ref, k_ref, q_ref, qnext_ref, v_ref, bias_ref, lq1_ref, lk1_ref, lq2_ref, lk2_ref,
                 g_ref, o_ref, qpad_ref, m_ref, acc_ref, st_ref, kbuf_ref):
    i = pl.program_id(1)
    d = DIFF_HEAD_DIM
    cur, nxt = i & 1, (i + 1) & 1
    _stage_keys(k_ref, kbuf_ref, qpad_ref, i == 0)

    def fill_queries(slot, src_ref):
        for c in range(2 * DIFF_HEADS):
            r0 = d * (c % 4)
            qpad_ref[slot, c, r0:r0 + d] = src_ref[0, 0, d * c:d * (c + 1)]

    def score(c, j, rows, slot=cur):
        pair = c // 4
        kblk = kbuf_ref[pl.ds(pl.multiple_of(j * TK, TK), rows), 256 * pair:256 * (pair + 1)]
        return jnp.dot(kblk, qpad_ref[slot, c], preferred_element_type=jnp.float32)

    _first_step_prime(i == 0, st_ref, fill_queries, q_ref, score)
    fill_queries(nxt, qnext_ref)
    _flash_init(m_ref, acc_ref)

    def step(j, rows, kind, prefetch_j):
        if prefetch_j is None:
            prefetch = lambda c: score(c, 0, 2 * TK, nxt)
        else:
            prefetch = lambda c: score(c, prefetch_j, 2 * TK)

        def update(c, st):
            hd = c // 2
            vt = _value_rows(v_ref, j, rows, DV_ROWS * hd, DV_ROWS)
            far_bias = rel_ref[far_bucket, hd] * LOG2E
            if kind == "far":
                _flash_update(m_ref, acc_ref, c, st, vt, shift=far_bias)
            else:
                bias = bias_ref[hd, 2 * TK - min(rows, 2 * TK):]
                if rows > 2 * TK:
                    bias = jnp.concatenate(
                        [jnp.full((rows - 2 * TK, TQ), far_bias, jnp.float32), bias], axis=0)
                _flash_update(m_ref, acc_ref, c, bias + st, vt)

        _run_chains(2 * DIFF_HEADS, st_ref, score, update, j, rows, prefetch)

    _key_block_schedule(i, step)

    lam = (jnp.exp(jnp.sum(lq1_ref[...] * lk1_ref[...], axis=-1, keepdims=True))
           - jnp.exp(jnp.sum(lq2_ref[...] * lk2_ref[...], axis=-1, keepdims=True))
           + lambda_init)
    for hd in range(DIFF_HEADS):
        dv = 2 * d
        o0 = acc_ref[2 * hd, :dv] * (1.0 / acc_ref[2 * hd, dv:dv + 1])
        o1 = acc_ref[2 * hd + 1, :dv] * (1.0 / acc_ref[2 * hd + 1, dv:dv + 1])
        od = o0 - lam * o1
        ms = jnp.mean(od * od, axis=0, keepdims=True)
        y = (od * lax.rsqrt(ms + EPS)).T * g_ref[...]
        o_ref[0, :, 128 * hd:128 * (hd + 1)] = (y * (1.0 - lambda_init)).astype(o_ref.dtype)


def _diff_attention(kd, qd, vd, bias_tiles, far_bucket, rel_bias, lq1, lk1, lq2, lk2, g, lambda_init):
    B, S, _ = kd.shape
    nq = S // TQ
    nmaps = 2 * DIFF_HEADS
    vec = lambda n: pl.BlockSpec((1, n), lambda b, i: (0, 0))
    return pl.pallas_call(
        functools.partial(_diff_kernel, far_bucket, lambda_init),
        grid=(B, nq),
        in_specs=[
            pl.BlockSpec(memory_space=pltpu.SMEM),
            pl.BlockSpec((1, S, DIFF_WIDTH), lambda b, i: (b, 0, 0)),
            pl.BlockSpec((1, 1, DIFF_WIDTH, TQ), lambda b, i: (b, i, 0, 0)),
            pl.BlockSpec((1, 1, DIFF_WIDTH, TQ), lambda b, i: (b, jnp.minimum(i + 1, nq - 1), 0, 0)),
            pl.BlockSpec((1, nq, DIFF_HEADS * DV_ROWS, TK), lambda b, i: (b, 0, 0, 0)),
            pl.BlockSpec((DIFF_HEADS, 2 * TK, TQ), lambda b, i: (0, 0, 0)),
            vec(DIFF_HEAD_DIM), vec(DIFF_HEAD_DIM), vec(DIFF_HEAD_DIM), vec(DIFF_HEAD_DIM),
            vec(2 * DIFF_HEAD_DIM),
        ],
        out_specs=pl.BlockSpec((1, TQ, DIFF_WIDTH), lambda b, i: (b, i, 0)),
        out_shape=jax.ShapeDtypeStruct((B, S, DIFF_WIDTH), jnp.bfloat16),
        scratch_shapes=[
            pltpu.VMEM((2, nmaps, 256, TQ), jnp.bfloat16),
            pltpu.VMEM((nmaps, 1, TQ), jnp.float32),
            pltpu.VMEM((nmaps, DV_ROWS, TQ), jnp.float32),
            pltpu.VMEM((LOOKAHEAD, 2 * TK, TQ), jnp.float32),
            pltpu.VMEM((S, DIFF_WIDTH), jnp.bfloat16),
        ],
        compiler_params=pltpu.CompilerParams(
            dimension_semantics=("arbitrary", "arbitrary"), vmem_limit_bytes=VMEM_LIMIT),
        name="diff_attention",
    )(rel_bias, kd, qd, qd, vd, bias_tiles, lq1, lk1, lq2, lk2, g)


def _fox_kernel(k_ref, q_ref, xq_ref, qnext_ref, xqnext_ref, v_ref, o_ref,
                qpad_ref, m_ref, acc_ref, st_ref, kbuf_ref):
    i = pl.program_id(1)
    d = FOX_HEAD_DIM
    cur, nxt = i & 1, (i + 1) & 1
    _stage_keys(k_ref, kbuf_ref, qpad_ref, i == 0)

    def fill_queries(slot, src_refs):
        qsrc_ref, xsrc_ref = src_refs
        for hd in range(FOX_HEADS):
            a = hd % 2
            qpad_ref[slot, hd, d * a:d * (a + 1)] = qsrc_ref[0, 0, d * hd:d * (hd + 1)]
            qpad_ref[slot, hd, 128 + XROWS * a:128 + XROWS * (a + 1)] = (
                xsrc_ref[0, 0, XROWS * hd:XROWS * (hd + 1)])

    def score(hd, j, rows, slot=cur):
        u = hd // 2
        kblk = kbuf_ref[pl.ds(pl.multiple_of(j * TK, TK), rows), 256 * u:256 * (u + 1)]
        return jnp.dot(kblk, qpad_ref[slot, hd], preferred_element_type=jnp.float32)

    _first_step_prime(i == 0, st_ref, fill_queries, (q_ref, xq_ref), score)
    fill_queries(nxt, (qnext_ref, xqnext_ref))
    _flash_init(m_ref, acc_ref)

    def step(j, rows, kind, prefetch_j):
        if prefetch_j is None:
            prefetch = lambda hd: score(hd, 0, 2 * TK, nxt)
        else:
            prefetch = lambda hd: score(hd, prefetch_j, 2 * TK)

        def update(hd, st):
            if kind != "far":
                row = lax.broadcasted_iota(jnp.int32, st.shape, 0) - (rows - TK)
                col = lax.broadcasted_iota(jnp.int32, st.shape, 1)
                st = jnp.where(row > col, NEG, st)
            _flash_update(m_ref, acc_ref, hd, st, _value_rows(v_ref, j, rows, FV_ROWS * hd, FV_ROWS))

        _run_chains(FOX_HEADS, st_ref, score, update, j, rows, prefetch)

    _key_block_schedule(i, step)

    for u in range(FOX_UNITS):
        o = jnp.concatenate([acc_ref[hd, :d] * (1.0 / acc_ref[hd, d:d + 1]) for hd in (2 * u, 2 * u + 1)],
                            axis=0)
        o_ref[0, :, 128 * u:128 * (u + 1)] = o.T.astype(o_ref.dtype)


def _fox_attention(kf, qf, xq, vf):
    B, S, _ = kf.shape
    nq = S // TQ
    return pl.pallas_call(
        _fox_kernel,
        grid=(B, nq),
        in_specs=[
            pl.BlockSpec((1, S, 2 * FOX_WIDTH), lambda b, i: (b, 0, 0)),
            pl.BlockSpec((1, 1, FOX_WIDTH, TQ), lambda b, i: (b, i, 0, 0)),
            pl.BlockSpec((1, 1, FOX_HEADS * XROWS, TQ), lambda b, i: (b, i, 0, 0)),
            pl.BlockSpec((1, 1, FOX_WIDTH, TQ), lambda b, i: (b, jnp.minimum(i + 1, nq - 1), 0, 0)),
            pl.BlockSpec((1, 1, FOX_HEADS * XROWS, TQ), lambda b, i: (b, jnp.minimum(i + 1, nq - 1), 0, 0)),
            pl.BlockSpec((1, nq, FOX_HEADS * FV_ROWS, TK), lambda b, i: (b, 0, 0, 0)),
        ],
        out_specs=pl.BlockSpec((1, TQ, FOX_WIDTH), lambda b, i: (b, i, 0)),
        out_shape=jax.ShapeDtypeStruct((B, S, FOX_WIDTH), jnp.bfloat16),
        scratch_shapes=[
            pltpu.VMEM((2, FOX_HEADS, 256, TQ), jnp.bfloat16),
            pltpu.VMEM((FOX_HEADS, 1, TQ), jnp.float32),
            pltpu.VMEM((FOX_HEADS, FV_ROWS, TQ), jnp.float32),
            pltpu.VMEM((LOOKAHEAD, 2 * TK, TQ), jnp.float32),
            pltpu.VMEM((S, 2 * FOX_WIDTH), jnp.bfloat16),
        ],
        compiler_params=pltpu.CompilerParams(
            dimension_semantics=("arbitrary", "arbitrary"), vmem_limit_bytes=VMEM_LIMIT),
        name="fox_attention",
    )(kf, qf, xq, qf, xq, vf)


def _post_kernel(md_ref, mf_ref, x_ref, wod_ref, wof_ref, wup_ref, wdn_ref, g1_ref, g2_ref, g3_ref,
                 o_ref):
    subs = [slice(s * POST_SUB, (s + 1) * POST_SUB) for s in range(x_ref.shape[0] // POST_SUB)]
    attn = [jnp.dot(md_ref[rows], wod_ref[...], preferred_element_type=jnp.float32)
            + jnp.dot(mf_ref[rows], wof_ref[...], preferred_element_type=jnp.float32) for rows in subs]
    for rows, a in zip(subs, attn):
        x1 = x_ref[rows] + _rms(a, g1_ref[...])
        h2 = _rms(x1, g2_ref[...]).astype(jnp.bfloat16)
        acc = None
        for c in range(D_FF // FF_CHUNK):
            u = jnp.dot(h2, wup_ref[:, c * FF_CHUNK:(c + 1) * FF_CHUNK], preferred_element_type=jnp.float32)
            u = jnp.square(jnp.maximum(u, 0.0)).astype(jnp.bfloat16)
            part = jnp.dot(u, wdn_ref[c * FF_CHUNK:(c + 1) * FF_CHUNK, :], preferred_element_type=jnp.float32)
            acc = part if acc is None else acc + part
        o_ref[rows] = x1 + _rms(acc, g3_ref[...])


def _post(mix_d, mix_f, x2d, wod, wof, wup, wdn, g1, g2, g3):
    R, D = x2d.shape
    tm = TM_POST
    resident = lambda shape: pl.BlockSpec(shape, lambda r: (0,) * len(shape),
                                          pipeline_mode=pl.Buffered(1))
    return pl.pallas_call(
        _post_kernel,
        grid=(R // tm,),
        in_specs=[
            pl.BlockSpec((tm, DIFF_WIDTH), lambda r: (r, 0)),
            pl.BlockSpec((tm, FOX_WIDTH), lambda r: (r, 0)),
            pl.BlockSpec((tm, D), lambda r: (r, 0)),
            resident(wod.shape), resident(wof.shape), resident(wup.shape), resident(wdn.shape),
            resident((1, D)), resident((1, D)), resident((1, D)),
        ],
        out_specs=pl.BlockSpec((tm, D), lambda r: (r, 0)),
        out_shape=jax.ShapeDtypeStruct((R, D), jnp.float32),
        compiler_params=pltpu.CompilerParams(
            dimension_semantics=("parallel",), vmem_limit_bytes=VMEM_LIMIT),
        name="post",
    )(mix_d, mix_f, x2d, wod, wof, wup, wdn, g1, g2, g3)


def kernel(x, ln_attn_pre, w_in, b_f, lam_q1, lam_k1, lam_q2, lam_k2, subln_g, rel_bias,
           w_out, ln_attn_post, ln_mlp_pre, w_up, w_down, ln_mlp_post):
    B, S, D = x.shape
    depth = w_in.shape[0]
    bf16 = jnp.bfloat16
    bias_tiles, far_bucket = _bias_tiles(rel_bias, S)
    dw, fw = DIFF_WIDTH, FOX_WIDTH
    for l in range(depth):
        lambda_init = 0.8 - 0.6 * math.exp(-0.3 * l)
        w = w_in[l]
        o_dq, o_dk, o_dv, o_fq, o_fk, o_fv, o_g = 0, dw, 2 * dw, 3 * dw, 3 * dw + fw, 3 * dw + 2 * fw, 3 * dw + 3 * fw
        wrow = jnp.concatenate([w[:, o_dk:o_dk + dw], w[:, o_fk:o_fk + fw]], axis=1).astype(bf16)
        wcol = jnp.concatenate([w[:, o_dq:o_dq + dw], w[:, o_dv:o_dv + dw],
                                w[:, o_fq:o_fq + fw], w[:, o_fv:o_fv + fw]], axis=1).T.astype(bf16)
        wf = jnp.zeros((XROWS, D), bf16).at[:FOX_HEADS].set(w[:, o_g:o_g + FOX_HEADS].T.astype(bf16))
        bf_col = jnp.zeros((XROWS, 1), jnp.float32).at[:FOX_HEADS, 0].set(b_f[l].astype(jnp.float32))

        kd, kf, qd, vd, qf, vf, xq = _proj(x, ln_attn_pre[l][None, :], wrow, wcol, wf, bf_col)
        mix_d = _diff_attention(kd, qd, vd, bias_tiles, far_bucket, rel_bias,
                                lam_q1[l][None, :], lam_k1[l][None, :], lam_q2[l][None, :],
                                lam_k2[l][None, :], subln_g[l][None, :], lambda_init)
        mix_f = _fox_attention(kf, qf, xq, vf)
        wo = w_out[l].astype(bf16)
        y = _post(mix_d.reshape(B * S, dw), mix_f.reshape(B * S, fw), x.reshape(B * S, D),
                  wo[:dw], wo[dw:], w_up[l].astype(bf16), w_down[l].astype(bf16),
                  ln_attn_post[l][None, :], ln_mlp_pre[l][None, :], ln_mlp_post[l][None, :])
        x = y.reshape(B, S, D)
    return x
```

```python
import functools
import math

import numpy as np
import jax
import jax.numpy as jnp
from jax import lax
from jax.experimental import pallas as pl
from jax.experimental.pallas import tpu as pltpu

D_MODEL = 1024
DIFF_HEADS = 4
DIFF_HEAD_DIM = 64
DIFF_WIDTH = DIFF_HEADS * 2 * DIFF_HEAD_DIM
FOX_HEADS = 8
FOX_HEAD_DIM = 64
FOX_WIDTH = FOX_HEADS * FOX_HEAD_DIM
FOX_UNITS = FOX_HEADS // 2
D_FF = 4 * D_MODEL
N_BUCKETS = 32
MAX_DISTANCE = 128
EPS = 1e-6
NEG = -1e30
QK_SCALE = DIFF_HEAD_DIM ** -0.5
LOG2E = math.log2(math.e)

TQ = 256
TK = 256
TM_PROJ = 1024
TM_POST = 1024
POST_SUB = 512
FF_CHUNK = 1024
LOOKAHEAD = 6
ONES_ROWS = 16
DV_ROWS = 2 * DIFF_HEAD_DIM + ONES_ROWS
FV_ROWS = FOX_HEAD_DIM + ONES_ROWS
XROWS = 16
VMEM_LIMIT = 56 * 1024 * 1024


def _rms(x, g):
    ms = jnp.mean(x * x, axis=-1, keepdims=True)
    return x * lax.rsqrt(ms + EPS) * g


def _proj_kernel(x_ref, g_ref, wrow_ref, wcol_ref, wf_ref, bf_ref, pq_ref, pk_ref, oq_ref, ok_ref,
                 kd_ref, kf_ref, qd_ref, vd_ref, qf_ref, vf_ref, xq_ref, carry_ref):
    tm = x_ref.shape[1]
    nsub = tm // TQ
    nt = (((1,), (1,)), ((), ()))
    ones = jnp.ones((ONES_ROWS, TQ), jnp.bfloat16)
    outs = ((qd_ref, None, None), (vd_ref, 2 * DIFF_HEAD_DIM, DV_ROWS),
            (qf_ref, None, None), (vf_ref, FOX_HEAD_DIM, FV_ROWS))

    def column_outputs(c, h, n):
        o_ref, hrows, orows = outs[n]
        colt = lax.dot_general(wcol_ref[n * 512:(n + 1) * 512, :], h, nt,
                               preferred_element_type=jnp.float32)
        if hrows is None:
            o_ref[0, c] = (colt * (QK_SCALE * LOG2E)).astype(jnp.bfloat16)
        else:
            colt = colt.astype(jnp.bfloat16)
            for hd in range(512 // hrows):
                o_ref[0, c, orows * hd:orows * hd + hrows] = colt[hrows * hd:hrows * (hd + 1)]
                o_ref[0, c, orows * hd + hrows:orows * (hd + 1)] = ones

    ft_parts = []
    for c in range(nsub):
        rows = slice(c * TQ, (c + 1) * TQ)
        h = _rms(x_ref[0, rows], g_ref[...]).astype(jnp.bfloat16)
        ft_parts.append(lax.dot_general(wf_ref[...], h, nt, preferred_element_type=jnp.float32))
        krow = jnp.dot(h, wrow_ref[...], preferred_element_type=jnp.float32)
        kd_ref[0, rows] = krow[:, :DIFF_WIDTH].astype(jnp.bfloat16)
        for u in range(FOX_UNITS):
            kf_ref[0, rows, 256 * u:256 * u + 128] = (
                krow[:, DIFF_WIDTH + 128 * u:DIFF_WIDTH + 128 * (u + 1)].astype(jnp.bfloat16))
        for n in range(len(outs)):
            column_outputs(c, h, n)

    ft = jnp.concatenate(ft_parts, axis=1) + bf_ref[...]
    ls = jnp.minimum(ft, 0.0) - jnp.log1p(jnp.exp(-jnp.abs(ft)))
    lane = lax.broadcasted_iota(jnp.int32, ls.shape, 1)
    sh = 1
    while sh < tm:
        ls = ls + jnp.where(lane >= sh, pltpu.roll(ls, sh, axis=1), 0.0)
        sh *= 2

    @pl.when(pl.program_id(1) == 0)
    def _():
        carry_ref[...] = jnp.zeros_like(carry_ref)

    cum = ls + carry_ref[...]
    carry_ref[...] = cum[:, tm - 1:tm]
    cum = cum * LOG2E

    hi = cum.astype(jnp.bfloat16)
    r1 = cum - hi.astype(jnp.float32)
    mid = r1.astype(jnp.bfloat16)
    lo = (r1 - mid.astype(jnp.float32)).astype(jnp.bfloat16)
    parts = jnp.concatenate([hi, mid, lo, jnp.zeros((128 - 3 * XROWS, tm), jnp.bfloat16)], axis=0)
    xq = jnp.dot(pq_ref[...], parts, preferred_element_type=jnp.float32) + oq_ref[...]
    xq = xq.astype(jnp.bfloat16)
    for c in range(nsub):
        xq_ref[0, c] = xq[:, c * TQ:(c + 1) * TQ]
    xk = jnp.dot(pk_ref[...], parts, preferred_element_type=jnp.float32) + ok_ref[...]
    for u in range(FOX_UNITS):
        kf_ref[0, :, 256 * u + 128:256 * (u + 1)] = (
            xk[128 * u:128 * (u + 1), :].T.astype(jnp.bfloat16))


def _placement_constants():
    pq = np.zeros((FOX_UNITS * 2 * XROWS, 128), np.float32)
    oq = np.zeros((FOX_UNITS * 2 * XROWS, 1), np.float32)
    pk = np.zeros((FOX_UNITS * 128, 128), np.float32)
    ok = np.zeros((FOX_UNITS * 128, 1), np.float32)
    for u in range(FOX_UNITS):
        for a in range(2):
            head = 2 * u + a
            for r in range(3):
                pq[2 * XROWS * u + XROWS * a + r, XROWS * r + head] = 1.0
                oq[2 * XROWS * u + XROWS * a + 3 + r, 0] = 1.0
                pk[128 * u + XROWS * a + 3 + r, XROWS * r + head] = -1.0
                ok[128 * u + XROWS * a + r, 0] = 1.0
    return (jnp.asarray(pq, jnp.bfloat16), jnp.asarray(pk, jnp.bfloat16),
            jnp.asarray(oq), jnp.asarray(ok))


def _proj(x, g, wrow, wcol, wf, bf_col):
    B, S, D = x.shape
    tm = TM_PROJ
    nsub = tm // TQ
    nq = S // TQ
    pq, pk, oq, ok = _placement_constants()
    const = lambda shape: pl.BlockSpec(shape, lambda b, t: (0,) * len(shape))
    colspec = lambda rows: pl.BlockSpec((1, nsub, rows, TQ), lambda b, t: (b, t, 0, 0))
    bf16 = jnp.bfloat16
    return pl.pallas_call(
        _proj_kernel,
        grid=(B, S // tm),
        in_specs=[
            pl.BlockSpec((1, tm, D), lambda b, t: (b, t, 0)),
            const((1, D)), const(wrow.shape), const(wcol.shape), const(wf.shape), const(bf_col.shape),
            const(pq.shape), const(pk.shape), const(oq.shape), const(ok.shape),
        ],
        out_specs=[
            pl.BlockSpec((1, tm, DIFF_WIDTH), lambda b, t: (b, t, 0)),
            pl.BlockSpec((1, tm, 2 * FOX_WIDTH), lambda b, t: (b, t, 0)),
            colspec(512), colspec(DIFF_HEADS * DV_ROWS), colspec(512), colspec(FOX_HEADS * FV_ROWS),
            colspec(FOX_UNITS * 2 * XROWS),
        ],
        out_shape=[
            jax.ShapeDtypeStruct((B, S, DIFF_WIDTH), bf16),
            jax.ShapeDtypeStruct((B, S, 2 * FOX_WIDTH), bf16),
            jax.ShapeDtypeStruct((B, nq, 512, TQ), bf16),
            jax.ShapeDtypeStruct((B, nq, DIFF_HEADS * DV_ROWS, TQ), bf16),
            jax.ShapeDtypeStruct((B, nq, 512, TQ), bf16),
            jax.ShapeDtypeStruct((B, nq, FOX_HEADS * FV_ROWS, TQ), bf16),
            jax.ShapeDtypeStruct((B, nq, FOX_UNITS * 2 * XROWS, TQ), bf16),
        ],
        scratch_shapes=[pltpu.VMEM((XROWS, 1), jnp.float32)],
        compiler_params=pltpu.CompilerParams(
            dimension_semantics=("arbitrary", "arbitrary"), vmem_limit_bytes=VMEM_LIMIT),
        name="proj",
    )(x, g, wrow, wcol, wf, bf_col, pq, pk, oq, ok)


def _t5_buckets(n):
    max_exact = N_BUCKETS // 2
    dist = np.arange(n)
    d = np.maximum(dist, 1).astype(np.float32)
    large = np.float32(max_exact) + (np.log(d / np.float32(max_exact))
                                     / np.float32(math.log(MAX_DISTANCE / max_exact))
                                     * np.float32(N_BUCKETS - max_exact))
    large = np.minimum(large.astype(np.int32), N_BUCKETS - 1)
    return np.where(dist < max_exact, dist, large).astype(np.int32)


def _bias_kernel(rel_ref, idx_ref, out_ref):
    hd = pl.program_id(0)
    idx = idx_ref[...]
    val = jnp.full(idx.shape, NEG, jnp.float32)
    for bkt in range(N_BUCKETS):
        val = jnp.where(idx == bkt, rel_ref[bkt, hd] * LOG2E, val)
    out_ref[0] = val


def _bias_tiles(rel_bias, S):
    buckets = _t5_buckets(S)
    far = int(buckets[-1])
    assert np.all(buckets[TQ + 1:] == far)
    s_loc = np.arange(TK)[:, None]
    t_loc = np.arange(TQ)[None, :]
    rel0 = t_loc - s_loc
    idx0 = np.where(rel0 >= 0, buckets[np.maximum(rel0, 0)], -1)
    idx1 = buckets[TQ + t_loc - s_loc]
    idx = jnp.asarray(np.concatenate([idx1, idx0], axis=0).astype(np.int32))
    tiles = pl.pallas_call(
        _bias_kernel,
        grid=(DIFF_HEADS,),
        in_specs=[pl.BlockSpec(memory_space=pltpu.SMEM),
                  pl.BlockSpec((2 * TK, TQ), lambda h: (0, 0))],
        out_specs=pl.BlockSpec((1, 2 * TK, TQ), lambda h: (h, 0, 0)),
        out_shape=jax.ShapeDtypeStruct((DIFF_HEADS, 2 * TK, TQ), jnp.float32),
        name="t5_bias_tiles",
    )(rel_bias, idx)
    return tiles, far


def _flash_update(m_ref, acc_ref, a, st, vt, shift=None):
    m_old = m_ref[a]
    mx = jnp.max(st, axis=0, keepdims=True)
    if shift is None:
        m_new = jnp.maximum(m_old, mx)
        x = st - m_new
    else:
        m_new = jnp.maximum(m_old, mx + shift)
        x = st - (m_new - shift)
    p = jnp.exp2(x.astype(jnp.bfloat16))
    alpha = jnp.exp2(m_old - m_new)
    acc_ref[a] = alpha * acc_ref[a] + jnp.dot(vt, p, preferred_element_type=jnp.float32)
    m_ref[a] = m_new


def _flash_init(m_ref, acc_ref):
    m_ref[...] = jnp.full(m_ref.shape, NEG, jnp.float32)
    acc_ref[...] = jnp.zeros_like(acc_ref)


def _run_chains(n, st_ref, score_fn, update_fn, j, rows, prefetch_fn, done_fn=None):
    def finish(item):
        update_fn(*item)
        if done_fn is not None:
            done_fn(item[0])

    def staged(c):
        if rows <= 2 * TK:
            return st_ref[c, :rows]
        return jnp.concatenate([st_ref[c], score_fn(c, j + 2, rows - 2 * TK)], axis=0)

    pending = [(c, staged(c)) for c in range(LOOKAHEAD)]
    for c in range(LOOKAHEAD, n):
        pending.append((c, score_fn(c, j, rows)))
        finish(pending.pop(0))
    for c in range(LOOKAHEAD):
        st_ref[c] = prefetch_fn(c)
        finish(pending.pop(0))


def _value_rows(v_ref, j, rows, r0, nrows):
    blocks = [v_ref[0, j + b, r0:r0 + nrows] for b in range(rows // TK)]
    return blocks[0] if len(blocks) == 1 else jnp.concatenate(blocks, axis=1)


def _stage_keys(k_ref, kbuf_ref, qpad_ref, first):
    @pl.when(first)
    def _():
        def body(t, carry):
            r = pl.multiple_of(t * TK, TK)
            kbuf_ref[pl.ds(r, TK), :] = k_ref[0, pl.ds(r, TK), :]
            return carry

        lax.fori_loop(0, k_ref.shape[1] // TK, body, 0)
        qpad_ref[...] = jnp.zeros_like(qpad_ref)


def _first_step_prime(first, st_ref, fill_queries, q_ref, score_fn):
    @pl.when(first)
    def _():
        fill_queries(0, q_ref)
        for c in range(LOOKAHEAD):
            st_ref[c] = score_fn(c, 0, 2 * TK, 0)


def _key_block_schedule(i, step):
    nfar = jnp.maximum(i - 1, 0)
    odd = (nfar & 1) == 1

    def far_pair(jj, carry):
        step(2 * jj, 2 * TK, "far", 2 * jj + 2)
        return carry

    lax.fori_loop(0, lax.shift_right_logical(nfar, 1), far_pair, 0)

    @pl.when(odd)
    def _():
        step(i - 2, 3 * TK, "last", None)

    @pl.when(jnp.logical_and(i >= 1, jnp.logical_not(odd)))
    def _():
        step(i - 1, 2 * TK, "last", None)

    @pl.when(i == 0)
    def _():
        step(0, TK, "last", None)


def _diff_kernel(far_bucket, lambda_init,
                 rel_ref, k_ref, q_ref, qnext_ref, v_ref, bias_ref, lq1_ref, lk1_ref, lq2_ref, lk2_ref,
                 g_ref, o_ref, qpad_ref, m_ref, acc_ref, st_ref, kbuf_ref):
    i = pl.program_id(1)
    d = DIFF_HEAD_DIM
    cur, nxt = i & 1, (i + 1) & 1
    _stage_keys(k_ref, kbuf_ref, qpad_ref, i == 0)

    def fill_queries(slot, src_ref):
        for c in range(2 * DIFF_HEADS):
            r0 = d * (c % 4)
            qpad_ref[slot, c, r0:r0 + d] = src_ref[0, 0, d * c:d * (c + 1)]

    def score(c, j, rows, slot=cur):
        pair = c // 4
        kblk = kbuf_ref[pl.ds(pl.multiple_of(j * TK, TK), rows), 256 * pair:256 * (pair + 1)]
        return jnp.dot(kblk, qpad_ref[slot, c], preferred_element_type=jnp.float32)

    _first_step_prime(i == 0, st_ref, fill_queries, q_ref, score)
    fill_queries(nxt, qnext_ref)
    _flash_init(m_ref, acc_ref)

    def step(j, rows, kind, prefetch_j):
        if prefetch_j is None:
            prefetch = lambda c: score(c, 0, 2 * TK, nxt)
        else:
            prefetch = lambda c: score(c, prefetch_j, 2 * TK)

        def update(c, st):
            hd = c // 2
            vt = _value_rows(v_ref, j, rows, DV_ROWS * hd, DV_ROWS)
            far_bias = rel_ref[far_bucket, hd] * LOG2E
            if kind == "far":
                _flash_update(m_ref, acc_ref, c, st, vt, shift=far_bias)
            else:
                bias = bias_ref[hd, 2 * TK - min(rows, 2 * TK):]
                if rows > 2 * TK:
                    bias = jnp.concatenate(
                        [jnp.full((rows - 2 * TK, TQ), far_bias, jnp.float32), bias], axis=0)
                _flash_update(m_ref, acc_ref, c, bias + st, vt)

        done = (lambda c: emit_head(c // 2) if c % 2 == 1 else None) if prefetch_j is None else None
        _run_chains(2 * DIFF_HEADS, st_ref, score, update, j, rows, prefetch, done)

    def emit_head(hd):
        lam = (jnp.exp(jnp.sum(lq1_ref[...] * lk1_ref[...], axis=-1, keepdims=True))
               - jnp.exp(jnp.sum(lq2_ref[...] * lk2_ref[...], axis=-1, keepdims=True))
               + lambda_init)
        dv = 2 * d
        o0 = acc_ref[2 * hd, :dv] * (1.0 / acc_ref[2 * hd, dv:dv + 1])
        o1 = acc_ref[2 * hd + 1, :dv] * (1.0 / acc_ref[2 * hd + 1, dv:dv + 1])
        od = o0 - lam * o1
        ms = jnp.mean(od * od, axis=0, keepdims=True)
        y = (od * lax.rsqrt(ms + EPS)).T * g_ref[...]
        o_ref[0, :, 128 * hd:128 * (hd + 1)] = (y * (1.0 - lambda_init)).astype(o_ref.dtype)

    _key_block_schedule(i, step)


def _diff_attention(kd, qd, vd, bias_tiles, far_bucket, rel_bias, lq1, lk1, lq2, lk2, g, lambda_init):
    B, S, _ = kd.shape
    nq = S // TQ
    nmaps = 2 * DIFF_HEADS
    vec = lambda n: pl.BlockSpec((1, n), lambda b, i: (0, 0))
    return pl.pallas_call(
        functools.partial(_diff_kernel, far_bucket, lambda_init),
        grid=(B, nq),
        in_specs=[
            pl.BlockSpec(memory_space=pltpu.SMEM),
            pl.BlockSpec((1, S, DIFF_WIDTH), lambda b, i: (b, 0, 0)),
            pl.BlockSpec((1, 1, DIFF_WIDTH, TQ), lambda b, i: (b, i, 0, 0)),
            pl.BlockSpec((1, 1, DIFF_WIDTH, TQ), lambda b, i: (b, jnp.minimum(i + 1, nq - 1), 0, 0)),
            pl.BlockSpec((1, nq, DIFF_HEADS * DV_ROWS, TK), lambda b, i: (b, 0, 0, 0)),
            pl.BlockSpec((DIFF_HEADS, 2 * TK, TQ), lambda b, i: (0, 0, 0)),
            vec(DIFF_HEAD_DIM), vec(DIFF_HEAD_DIM), vec(DIFF_HEAD_DIM), vec(DIFF_HEAD_DIM),
            vec(2 * DIFF_HEAD_DIM),
        ],
        out_specs=pl.BlockSpec((1, TQ, DIFF_WIDTH), lambda b, i: (b, i, 0)),
        out_shape=jax.ShapeDtypeStruct((B, S, DIFF_WIDTH), jnp.bfloat16),
        scratch_shapes=[
            pltpu.VMEM((2, nmaps, 256, TQ), jnp.bfloat16),
            pltpu.VMEM((nmaps, 1, TQ), jnp.float32),
            pltpu.VMEM((nmaps, DV_ROWS, TQ), jnp.float32),
            pltpu.VMEM((LOOKAHEAD, 2 * TK, TQ), jnp.float32),
            pltpu.VMEM((S, DIFF_WIDTH), jnp.bfloat16),
        ],
        compiler_params=pltpu.CompilerParams(
            dimension_semantics=("arbitrary", "arbitrary"), vmem_limit_bytes=VMEM_LIMIT),
        name="diff_attention",
    )(rel_bias, kd, qd, qd, vd, bias_tiles, lq1, lk1, lq2, lk2, g)


def _fox_kernel(k_ref, q_ref, xq_ref, qnext_ref, xqnext_ref, v_ref, o_ref,
                qpad_ref, m_ref, acc_ref, st_ref, kbuf_ref):
    i = pl.program_id(1)
    d = FOX_HEAD_DIM
    cur, nxt = i & 1, (i + 1) & 1
    _stage_keys(k_ref, kbuf_ref, qpad_ref, i == 0)

    def fill_queries(slot, src_refs):
        qsrc_ref, xsrc_ref = src_refs
        for hd in range(FOX_HEADS):
            a = hd % 2
            qpad_ref[slot, hd, d * a:d * (a + 1)] = qsrc_ref[0, 0, d * hd:d * (hd + 1)]
            qpad_ref[slot, hd, 128 + XROWS * a:128 + XROWS * (a + 1)] = (
                xsrc_ref[0, 0, XROWS * hd:XROWS * (hd + 1)])

    def score(hd, j, rows, slot=cur):
        u = hd // 2
        kblk = kbuf_ref[pl.ds(pl.multiple_of(j * TK, TK), rows), 256 * u:256 * (u + 1)]
        return jnp.dot(kblk, qpad_ref[slot, hd], preferred_element_type=jnp.float32)

    _first_step_prime(i == 0, st_ref, fill_queries, (q_ref, xq_ref), score)
    fill_queries(nxt, (qnext_ref, xqnext_ref))
    _flash_init(m_ref, acc_ref)

    def step(j, rows, kind, prefetch_j):
        if prefetch_j is None:
            prefetch = lambda hd: score(hd, 0, 2 * TK, nxt)
        else:
            prefetch = lambda hd: score(hd, prefetch_j, 2 * TK)

        def update(hd, st):
            if kind != "far":
                row = lax.broadcasted_iota(jnp.int32, st.shape, 0) - (rows - TK)
                col = lax.broadcasted_iota(jnp.int32, st.shape, 1)
                st = jnp.where(row > col, NEG, st)
            _flash_update(m_ref, acc_ref, hd, st, _value_rows(v_ref, j, rows, FV_ROWS * hd, FV_ROWS))

        done = (lambda hd: emit_pair(hd // 2) if hd % 2 == 1 else None) if prefetch_j is None else None
        _run_chains(FOX_HEADS, st_ref, score, update, j, rows, prefetch, done)

    def emit_pair(u):
        o = jnp.concatenate([acc_ref[hd, :d] * (1.0 / acc_ref[hd, d:d + 1]) for hd in (2 * u, 2 * u + 1)],
                            axis=0)
        o_ref[0, :, 128 * u:128 * (u + 1)] = o.T.astype(o_ref.dtype)

    _key_block_schedule(i, step)


def _fox_attention(kf, qf, xq, vf):
    B, S, _ = kf.shape
    nq = S // TQ
    return pl.pallas_call(
        _fox_kernel,
        grid=(B, nq),
        in_specs=[
            pl.BlockSpec((1, S, 2 * FOX_WIDTH), lambda b, i: (b, 0, 0)),
            pl.BlockSpec((1, 1, FOX_WIDTH, TQ), lambda b, i: (b, i, 0, 0)),
            pl.BlockSpec((1, 1, FOX_HEADS * XROWS, TQ), lambda b, i: (b, i, 0, 0)),
            pl.BlockSpec((1, 1, FOX_WIDTH, TQ), lambda b, i: (b, jnp.minimum(i + 1, nq - 1), 0, 0)),
            pl.BlockSpec((1, 1, FOX_HEADS * XROWS, TQ), lambda b, i: (b, jnp.minimum(i + 1, nq - 1), 0, 0)),
            pl.BlockSpec((1, nq, FOX_HEADS * FV_ROWS, TK), lambda b, i: (b, 0, 0, 0)),
        ],
        out_specs=pl.BlockSpec((1, TQ, FOX_WIDTH), lambda b, i: (b, i, 0)),
        out_shape=jax.ShapeDtypeStruct((B, S, FOX_WIDTH), jnp.bfloat16),
        scratch_shapes=[
            pltpu.VMEM((2, FOX_HEADS, 256, TQ), jnp.bfloat16),
            pltpu.VMEM((FOX_HEADS, 1, TQ), jnp.float32),
            pltpu.VMEM((FOX_HEADS, FV_ROWS, TQ), jnp.float32),
            pltpu.VMEM((LOOKAHEAD, 2 * TK, TQ), jnp.float32),
            pltpu.VMEM((S, 2 * FOX_WIDTH), jnp.bfloat16),
        ],
        compiler_params=pltpu.CompilerParams(
            dimension_semantics=("arbitrary", "arbitrary"), vmem_limit_bytes=VMEM_LIMIT),
        name="fox_attention",
    )(kf, qf, xq, qf, xq, vf)


def _post_kernel(md_ref, mf_ref, x_ref, wod_ref, wof_ref, wup_ref, wdn_ref, g1_ref, g2_ref, g3_ref,
                 o_ref):
    subs = [slice(s * POST_SUB, (s + 1) * POST_SUB) for s in range(x_ref.shape[0] // POST_SUB)]
    attn = [jnp.dot(md_ref[rows], wod_ref[...], preferred_element_type=jnp.float32)
            + jnp.dot(mf_ref[rows], wof_ref[...], preferred_element_type=jnp.float32) for rows in subs]
    for rows, a in zip(subs, attn):
        x1 = x_ref[rows] + _rms(a, g1_ref[...])
        h2 = _rms(x1, g2_ref[...]).astype(jnp.bfloat16)
        acc = None
        for c in range(D_FF // FF_CHUNK):
            u = jnp.dot(h2, wup_ref[:, c * FF_CHUNK:(c + 1) * FF_CHUNK], preferred_element_type=jnp.float32)
            u = jnp.square(jnp.maximum(u, 0.0)).astype(jnp.bfloat16)
            part = jnp.dot(u, wdn_ref[c * FF_CHUNK:(c + 1) * FF_CHUNK, :], preferred_element_type=jnp.float32)
            acc = part if acc is None else acc + part
        o_ref[rows] = x1 + _rms(acc, g3_ref[...])


def _post(mix_d, mix_f, x2d, wod, wof, wup, wdn, g1, g2, g3):
    R, D = x2d.shape
    tm = TM_POST
    resident = lambda shape: pl.BlockSpec(shape, lambda r: (0,) * len(shape),
                                          pipeline_mode=pl.Buffered(1))
    return pl.pallas_call(
        _post_kernel,
        grid=(R // tm,),
        in_specs=[
            pl.BlockSpec((tm, DIFF_WIDTH), lambda r: (r, 0)),
            pl.BlockSpec((tm, FOX_WIDTH), lambda r: (r, 0)),
            pl.BlockSpec((tm, D), lambda r: (r, 0)),
            resident(wod.shape), resident(wof.shape), resident(wup.shape), resident(wdn.shape),
            resident((1, D)), resident((1, D)), resident((1, D)),
        ],
        out_specs=pl.BlockSpec((tm, D), lambda r: (r, 0)),
        out_shape=jax.ShapeDtypeStruct((R, D), jnp.float32),
        compiler_params=pltpu.CompilerParams(
            dimension_semantics=("parallel",), vmem_limit_bytes=VMEM_LIMIT),
        name="post",
    )(mix_d, mix_f, x2d, wod, wof, wup, wdn, g1, g2, g3)


def kernel(x, ln_attn_pre, w_in, b_f, lam_q1, lam_k1, lam_q2, lam_k2, subln_g, rel_bias,
           w_out, ln_attn_post, ln_mlp_pre, w_up, w_down, ln_mlp_post):
    B, S, D = x.shape
    depth = w_in.shape[0]
    bf16 = jnp.bfloat16
    bias_tiles, far_bucket = _bias_tiles(rel_bias, S)
    dw, fw = DIFF_WIDTH, FOX_WIDTH
    for l in range(depth):
        lambda_init = 0.8 - 0.6 * math.exp(-0.3 * l)
        w = w_in[l]
        o_dq, o_dk, o_dv, o_fq, o_fk, o_fv, o_g = 0, dw, 2 * dw, 3 * dw, 3 * dw + fw, 3 * dw + 2 * fw, 3 * dw + 3 * fw
        wrow = jnp.concatenate([w[:, o_dk:o_dk + dw], w[:, o_fk:o_fk + fw]], axis=1).astype(bf16)
        wcol = jnp.concatenate([w[:, o_dq:o_dq + dw], w[:, o_dv:o_dv + dw],
                                w[:, o_fq:o_fq + fw], w[:, o_fv:o_fv + fw]], axis=1).T.astype(bf16)
        wf = jnp.zeros((XROWS, D), bf16).at[:FOX_HEADS].set(w[:, o_g:o_g + FOX_HEADS].T.astype(bf16))
        bf_col = jnp.zeros((XROWS, 1), jnp.float32).at[:FOX_HEADS, 0].set(b_f[l].astype(jnp.float32))

        kd, kf, qd, vd, qf, vf, xq = _proj(x, ln_attn_pre[l][None, :], wrow, wcol, wf, bf_col)
        mix_d = _diff_attention(kd, qd, vd, bias_tiles, far_bucket, rel_bias,
                                lam_q1[l][None, :], lam_k1[l][None, :], lam_q2[l][None, :],
                                lam_k2[l][None, :], subln_g[l][None, :], lambda_init)
        mix_f = _fox_attention(kf, qf, xq, vf)
        wo = w_out[l].astype(bf16)
        y = _post(mix_d.reshape(B * S, dw), mix_f.reshape(B * S, fw), x.reshape(B * S, D),
                  wo[:dw], wo[dw:], w_up[l].astype(bf16), w_down[l].astype(bf16),
                  ln_attn_post[l][None, :], ln_mlp_pre[l][None, :], ln_mlp_post[l][None, :])
        x = y.reshape(B, S, D)
    return x
```

```python
import functools
import math

import numpy as np
import jax
import jax.numpy as jnp
from jax import lax
from jax.experimental import pallas as pl
from jax.experimental.pallas import tpu as pltpu

D_MODEL = 1024
DIFF_HEADS = 4
DIFF_HEAD_DIM = 64
DIFF_WIDTH = DIFF_HEADS * 2 * DIFF_HEAD_DIM
FOX_HEADS = 8
FOX_HEAD_DIM = 64
FOX_WIDTH = FOX_HEADS * FOX_HEAD_DIM
FOX_UNITS = FOX_HEADS // 2
D_FF = 4 * D_MODEL
N_BUCKETS = 32
MAX_DISTANCE = 128
EPS = 1e-6
NEG = -1e30
QK_SCALE = DIFF_HEAD_DIM ** -0.5
LOG2E = math.log2(math.e)

TQ = 256
TK = 256
TM_PROJ = 1024
TM_POST = 1024
POST_SUB = 512
FF_CHUNK = 1024
LOOKAHEAD = 8
ONES_ROWS = 16
DV_ROWS = 2 * DIFF_HEAD_DIM + ONES_ROWS
FV_ROWS = FOX_HEAD_DIM + ONES_ROWS
XROWS = 16
VMEM_LIMIT = 56 * 1024 * 1024


def _rms(x, g):
    ms = jnp.mean(x * x, axis=-1, keepdims=True)
    return x * lax.rsqrt(ms + EPS) * g


def _proj_kernel(x_ref, g_ref, wrow_ref, wcol_ref, wf_ref, bf_ref, pq_ref, pk_ref, oq_ref, ok_ref,
                 kd_ref, kf_ref, qd_ref, vd_ref, qf_ref, vf_ref, xq_ref, carry_ref):
    tm = x_ref.shape[1]
    nsub = tm // TQ
    nt = (((1,), (1,)), ((), ()))
    ones = jnp.ones((ONES_ROWS, TQ), jnp.bfloat16)
    outs = ((qd_ref, None, None), (vd_ref, 2 * DIFF_HEAD_DIM, DV_ROWS),
            (qf_ref, None, None), (vf_ref, FOX_HEAD_DIM, FV_ROWS))

    def column_outputs(c, h, n):
        o_ref, hrows, orows = outs[n]
        colt = lax.dot_general(wcol_ref[n * 512:(n + 1) * 512, :], h, nt,
                               preferred_element_type=jnp.float32)
        if hrows is None:
            o_ref[0, c] = (colt * (QK_SCALE * LOG2E)).astype(jnp.bfloat16)
        else:
            colt = colt.astype(jnp.bfloat16)
            for hd in range(512 // hrows):
                o_ref[0, c, orows * hd:orows * hd + hrows] = colt[hrows * hd:hrows * (hd + 1)]
                o_ref[0, c, orows * hd + hrows:orows * (hd + 1)] = ones

    ft_parts = []
    for c in range(nsub):
        rows = slice(c * TQ, (c + 1) * TQ)
        h = _rms(x_ref[0, rows], g_ref[...]).astype(jnp.bfloat16)
        ft_parts.append(lax.dot_general(wf_ref[...], h, nt, preferred_element_type=jnp.float32))
        krow = jnp.dot(h, wrow_ref[...], preferred_element_type=jnp.float32)
        kd_ref[0, rows] = krow[:, :DIFF_WIDTH].astype(jnp.bfloat16)
        for u in range(FOX_UNITS):
            kf_ref[0, rows, 256 * u:256 * u + 128] = (
                krow[:, DIFF_WIDTH + 128 * u:DIFF_WIDTH + 128 * (u + 1)].astype(jnp.bfloat16))
        for n in range(len(outs)):
            column_outputs(c, h, n)

    ft = jnp.concatenate(ft_parts, axis=1) + bf_ref[...]
    ls = jnp.minimum(ft, 0.0) - jnp.log1p(jnp.exp(-jnp.abs(ft)))
    lane = lax.broadcasted_iota(jnp.int32, ls.shape, 1)
    sh = 1
    while sh < tm:
        ls = ls + jnp.where(lane >= sh, pltpu.roll(ls, sh, axis=1), 0.0)
        sh *= 2

    @pl.when(pl.program_id(1) == 0)
    def _():
        carry_ref[...] = jnp.zeros_like(carry_ref)

    cum = ls + carry_ref[...]
    carry_ref[...] = cum[:, tm - 1:tm]
    cum = cum * LOG2E

    hi = cum.astype(jnp.bfloat16)
    r1 = cum - hi.astype(jnp.float32)
    mid = r1.astype(jnp.bfloat16)
    lo = (r1 - mid.astype(jnp.float32)).astype(jnp.bfloat16)
    parts = jnp.concatenate([hi, mid, lo, jnp.zeros((128 - 3 * XROWS, tm), jnp.bfloat16)], axis=0)
    xq = jnp.dot(pq_ref[...], parts, preferred_element_type=jnp.float32) + oq_ref[...]
    xq = xq.astype(jnp.bfloat16)
    for c in range(nsub):
        xq_ref[0, c] = xq[:, c * TQ:(c + 1) * TQ]
    xk = jnp.dot(pk_ref[...], parts, preferred_element_type=jnp.float32) + ok_ref[...]
    for u in range(FOX_UNITS):
        kf_ref[0, :, 256 * u + 128:256 * (u + 1)] = (
            xk[128 * u:128 * (u + 1), :].T.astype(jnp.bfloat16))


def _placement_constants():
    pq = np.zeros((FOX_UNITS * 2 * XROWS, 128), np.float32)
    oq = np.zeros((FOX_UNITS * 2 * XROWS, 1), np.float32)
    pk = np.zeros((FOX_UNITS * 128, 128), np.float32)
    ok = np.zeros((FOX_UNITS * 128, 1), np.float32)
    for u in range(FOX_UNITS):
        for a in range(2):
            head = 2 * u + a
            for r in range(3):
                pq[2 * XROWS * u + XROWS * a + r, XROWS * r + head] = 1.0
                oq[2 * XROWS * u + XROWS * a + 3 + r, 0] = 1.0
                pk[128 * u + XROWS * a + 3 + r, XROWS * r + head] = -1.0
                ok[128 * u + XROWS * a + r, 0] = 1.0
    return (jnp.asarray(pq, jnp.bfloat16), jnp.asarray(pk, jnp.bfloat16),
            jnp.asarray(oq), jnp.asarray(ok))


def _proj(x, g, wrow, wcol, wf, bf_col):
    B, S, D = x.shape
    tm = TM_PROJ
    nsub = tm // TQ
    nq = S // TQ
    pq, pk, oq, ok = _placement_constants()
    const = lambda shape: pl.BlockSpec(shape, lambda b, t: (0,) * len(shape))
    colspec = lambda rows: pl.BlockSpec((1, nsub, rows, TQ), lambda b, t: (b, t, 0, 0))
    bf16 = jnp.bfloat16
    return pl.pallas_call(
        _proj_kernel,
        grid=(B, S // tm),
        in_specs=[
            pl.BlockSpec((1, tm, D), lambda b, t: (b, t, 0)),
            const((1, D)), const(wrow.shape), const(wcol.shape), const(wf.shape), const(bf_col.shape),
            const(pq.shape), const(pk.shape), const(oq.shape), const(ok.shape),
        ],
        out_specs=[
            pl.BlockSpec((1, tm, DIFF_WIDTH), lambda b, t: (b, t, 0)),
            pl.BlockSpec((1, tm, 2 * FOX_WIDTH), lambda b, t: (b, t, 0)),
            colspec(512), colspec(DIFF_HEADS * DV_ROWS), colspec(512), colspec(FOX_HEADS * FV_ROWS),
            colspec(FOX_UNITS * 2 * XROWS),
        ],
        out_shape=[
            jax.ShapeDtypeStruct((B, S, DIFF_WIDTH), bf16),
            jax.ShapeDtypeStruct((B, S, 2 * FOX_WIDTH), bf16),
            jax.ShapeDtypeStruct((B, nq, 512, TQ), bf16),
            jax.ShapeDtypeStruct((B, nq, DIFF_HEADS * DV_ROWS, TQ), bf16),
            jax.ShapeDtypeStruct((B, nq, 512, TQ), bf16),
            jax.ShapeDtypeStruct((B, nq, FOX_HEADS * FV_ROWS, TQ), bf16),
            jax.ShapeDtypeStruct((B, nq, FOX_UNITS * 2 * XROWS, TQ), bf16),
        ],
        scratch_shapes=[pltpu.VMEM((XROWS, 1), jnp.float32)],
        compiler_params=pltpu.CompilerParams(
            dimension_semantics=("arbitrary", "arbitrary"), vmem_limit_bytes=VMEM_LIMIT),
        name="proj",
    )(x, g, wrow, wcol, wf, bf_col, pq, pk, oq, ok)


def _t5_buckets(n):
    max_exact = N_BUCKETS // 2
    dist = np.arange(n)
    d = np.maximum(dist, 1).astype(np.float32)
    large = np.float32(max_exact) + (np.log(d / np.float32(max_exact))
                                     / np.float32(math.log(MAX_DISTANCE / max_exact))
                                     * np.float32(N_BUCKETS - max_exact))
    large = np.minimum(large.astype(np.int32), N_BUCKETS - 1)
    return np.where(dist < max_exact, dist, large).astype(np.int32)


def _bias_kernel(rel_ref, idx_ref, out_ref):
    hd = pl.program_id(0)
    idx = idx_ref[...]
    val = jnp.full(idx.shape, NEG, jnp.float32)
    for bkt in range(N_BUCKETS):
        val = jnp.where(idx == bkt, rel_ref[bkt, hd] * LOG2E, val)
    out_ref[0] = val


def _bias_tiles(rel_bias, S):
    buckets = _t5_buckets(S)
    far = int(buckets[-1])
    assert np.all(buckets[TQ + 1:] == far)
    s_loc = np.arange(TK)[:, None]
    t_loc = np.arange(TQ)[None, :]
    rel0 = t_loc - s_loc
    idx0 = np.where(rel0 >= 0, buckets[np.maximum(rel0, 0)], -1)
    idx1 = buckets[TQ + t_loc - s_loc]
    idx = jnp.asarray(np.concatenate([idx1, idx0], axis=0).astype(np.int32))
    tiles = pl.pallas_call(
        _bias_kernel,
        grid=(DIFF_HEADS,),
        in_specs=[pl.BlockSpec(memory_space=pltpu.SMEM),
                  pl.BlockSpec((2 * TK, TQ), lambda h: (0, 0))],
        out_specs=pl.BlockSpec((1, 2 * TK, TQ), lambda h: (h, 0, 0)),
        out_shape=jax.ShapeDtypeStruct((DIFF_HEADS, 2 * TK, TQ), jnp.float32),
        name="t5_bias_tiles",
    )(rel_bias, idx)
    return tiles, far


def _flash_update(m_ref, acc_ref, a, st, vt, shift=None):
    m_old = m_ref[a]
    mx = jnp.max(st, axis=0, keepdims=True)
    if shift is None:
        m_new = jnp.maximum(m_old, mx)
        x = st - m_new
    else:
        m_new = jnp.maximum(m_old, mx + shift)
        x = st - (m_new - shift)
    p = jnp.exp2(x.astype(jnp.bfloat16))
    alpha = jnp.exp2(m_old - m_new)
    acc_ref[a] = alpha * acc_ref[a] + jnp.dot(vt, p, preferred_element_type=jnp.float32)
    m_ref[a] = m_new


def _flash_init(m_ref, acc_ref):
    m_ref[...] = jnp.full(m_ref.shape, NEG, jnp.float32)
    acc_ref[...] = jnp.zeros_like(acc_ref)


def _run_chains(n, st_ref, score_fn, update_fn, j, rows, prefetch_fn, done_fn=None):
    def finish(item):
        update_fn(*item)
        if done_fn is not None:
            done_fn(item[0])

    def staged(c):
        if rows <= 2 * TK:
            return st_ref[c, :rows]
        return jnp.concatenate([st_ref[c], score_fn(c, j + 2, rows - 2 * TK)], axis=0)

    pending = [(c, staged(c)) for c in range(LOOKAHEAD)]
    for c in range(LOOKAHEAD, n):
        pending.append((c, score_fn(c, j, rows)))
        finish(pending.pop(0))
    for c in range(LOOKAHEAD):
        st_ref[c] = prefetch_fn(c)
        finish(pending.pop(0))


def _value_rows(v_ref, j, rows, r0, nrows):
    blocks = [v_ref[0, j + b, r0:r0 + nrows] for b in range(rows // TK)]
    return blocks[0] if len(blocks) == 1 else jnp.concatenate(blocks, axis=1)


def _stage_keys(k_ref, kbuf_ref, qpad_ref, first):
    @pl.when(first)
    def _():
        def body(t, carry):
            r = pl.multiple_of(t * TK, TK)
            kbuf_ref[pl.ds(r, TK), :] = k_ref[0, pl.ds(r, TK), :]
            return carry

        lax.fori_loop(0, k_ref.shape[1] // TK, body, 0)
        qpad_ref[...] = jnp.zeros_like(qpad_ref)


def _first_step_prime(first, st_ref, fill_queries, q_ref, score_fn):
    @pl.when(first)
    def _():
        fill_queries(0, q_ref)
        for c in range(LOOKAHEAD):
            st_ref[c] = score_fn(c, 0, 2 * TK, 0)


def _key_block_schedule(i, step):
    nfar = jnp.maximum(i - 1, 0)
    odd = (nfar & 1) == 1

    def far_pair(jj, carry):
        step(2 * jj, 2 * TK, "far", 2 * jj + 2)
        return carry

    lax.fori_loop(0, lax.shift_right_logical(nfar, 1), far_pair, 0)

    @pl.when(odd)
    def _():
        step(i - 2, 3 * TK, "last", None)

    @pl.when(jnp.logical_and(i >= 1, jnp.logical_not(odd)))
    def _():
        step(i - 1, 2 * TK, "last", None)

    @pl.when(i == 0)
    def _():
        step(0, TK, "last", None)


def _diff_kernel(far_bucket, lambda_init,
                 rel_ref, k_ref, q_ref, qnext_ref, v_ref, bias_ref, lq1_ref, lk1_ref, lq2_ref, lk2_ref,
                 g_ref, o_ref, qpad_ref, m_ref, acc_ref, st_ref, kbuf_ref):
    i = pl.program_id(1)
    d = DIFF_HEAD_DIM
    cur, nxt = i & 1, (i + 1) & 1
    _stage_keys(k_ref, kbuf_ref, qpad_ref, i == 0)

    def fill_queries(slot, src_ref):
        for c in range(2 * DIFF_HEADS):
            r0 = d * (c % 4)
            qpad_ref[slot, c, r0:r0 + d] = src_ref[0, 0, d * c:d * (c + 1)]

    def score(c, j, rows, slot=cur):
        pair = c // 4
        kblk = kbuf_ref[pl.ds(pl.multiple_of(j * TK, TK), rows), 256 * pair:256 * (pair + 1)]
        return jnp.dot(kblk, qpad_ref[slot, c], preferred_element_type=jnp.float32)

    _first_step_prime(i == 0, st_ref, fill_queries, q_ref, score)
    fill_queries(nxt, qnext_ref)
    _flash_init(m_ref, acc_ref)

    def step(j, rows, kind, prefetch_j):
        if prefetch_j is None:
            prefetch = lambda c: score(c, 0, 2 * TK, nxt)
        else:
            prefetch = lambda c: score(c, prefetch_j, 2 * TK)

        def update(c, st):
            hd = c // 2
            vt = _value_rows(v_ref, j, rows, DV_ROWS * hd, DV_ROWS)
            far_bias = rel_ref[far_bucket, hd] * LOG2E
            if kind == "far":
                _flash_update(m_ref, acc_ref, c, st, vt, shift=far_bias)
            else:
                bias = bias_ref[hd, 2 * TK - min(rows, 2 * TK):]
                if rows > 2 * TK:
                    bias = jnp.concatenate(
                        [jnp.full((rows - 2 * TK, TQ), far_bias, jnp.float32), bias], axis=0)
                _flash_update(m_ref, acc_ref, c, bias + st, vt)

        done = (lambda c: emit_head(c // 2) if c % 2 == 1 else None) if prefetch_j is None else None
        _run_chains(2 * DIFF_HEADS, st_ref, score, update, j, rows, prefetch, done)

    def emit_head(hd):
        lam = (jnp.exp(jnp.sum(lq1_ref[...] * lk1_ref[...], axis=-1, keepdims=True))
               - jnp.exp(jnp.sum(lq2_ref[...] * lk2_ref[...], axis=-1, keepdims=True))
               + lambda_init)
        dv = 2 * d
        o0 = acc_ref[2 * hd, :dv] * (1.0 / acc_ref[2 * hd, dv:dv + 1])
        o1 = acc_ref[2 * hd + 1, :dv] * (1.0 / acc_ref[2 * hd + 1, dv:dv + 1])
        od = o0 - lam * o1
        ms = jnp.mean(od * od, axis=0, keepdims=True)
        y = (od * lax.rsqrt(ms + EPS)).T * g_ref[...]
        o_ref[0, :, 128 * hd:128 * (hd + 1)] = (y * (1.0 - lambda_init)).astype(o_ref.dtype)

    _key_block_schedule(i, step)


def _diff_attention(kd, qd, vd, bias_tiles, far_bucket, rel_bias, lq1, lk1, lq2, lk2, g, lambda_init):
    B, S, _ = kd.shape
    nq = S // TQ
    nmaps = 2 * DIFF_HEADS
    vec = lambda n: pl.BlockSpec((1, n), lambda b, i: (0, 0))
    return pl.pallas_call(
        functools.partial(_diff_kernel, far_bucket, lambda_init),
        grid=(B, nq),
        in_specs=[
            pl.BlockSpec(memory_space=pltpu.SMEM),
            pl.BlockSpec((1, S, DIFF_WIDTH), lambda b, i: (b, 0, 0)),
            pl.BlockSpec((1, 1, DIFF_WIDTH, TQ), lambda b, i: (b, i, 0, 0)),
            pl.BlockSpec((1, 1, DIFF_WIDTH, TQ), lambda b, i: (b, jnp.minimum(i + 1, nq - 1), 0, 0)),
            pl.BlockSpec((1, nq, DIFF_HEADS * DV_ROWS, TK), lambda b, i: (b, 0, 0, 0)),
            pl.BlockSpec((DIFF_HEADS, 2 * TK, TQ), lambda b, i: (0, 0, 0)),
            vec(DIFF_HEAD_DIM), vec(DIFF_HEAD_DIM), vec(DIFF_HEAD_DIM), vec(DIFF_HEAD_DIM),
            vec(2 * DIFF_HEAD_DIM),
        ],
        out_specs=pl.BlockSpec((1, TQ, DIFF_WIDTH), lambda b, i: (b, i, 0)),
        out_shape=jax.ShapeDtypeStruct((B, S, DIFF_WIDTH), jnp.bfloat16),
        scratch_shapes=[
            pltpu.VMEM((2, nmaps, 256, TQ), jnp.bfloat16),
            pltpu.VMEM((nmaps, 1, TQ), jnp.float32),
            pltpu.VMEM((nmaps, DV_ROWS, TQ), jnp.float32),
            pltpu.VMEM((LOOKAHEAD, 2 * TK, TQ), jnp.float32),
            pltpu.VMEM((S, DIFF_WIDTH), jnp.bfloat16),
        ],
        compiler_params=pltpu.CompilerParams(
            dimension_semantics=("arbitrary", "arbitrary"), vmem_limit_bytes=VMEM_LIMIT),
        name="diff_attention",
    )(rel_bias, kd, qd, qd, vd, bias_tiles, lq1, lk1, lq2, lk2, g)


def _fox_kernel(k_ref, q_ref, xq_ref, qnext_ref, xqnext_ref, v_ref, o_ref,
                qpad_ref, m_ref, acc_ref, st_ref, kbuf_ref):
    i = pl.program_id(1)
    d = FOX_HEAD_DIM
    cur, nxt = i & 1, (i + 1) & 1
    _stage_keys(k_ref, kbuf_ref, qpad_ref, i == 0)

    def fill_queries(slot, src_refs):
        qsrc_ref, xsrc_ref = src_refs
        for hd in range(FOX_HEADS):
            a = hd % 2
            qpad_ref[slot, hd, d * a:d * (a + 1)] = qsrc_ref[0, 0, d * hd:d * (hd + 1)]
            qpad_ref[slot, hd, 128 + XROWS * a:128 + XROWS * (a + 1)] = (
                xsrc_ref[0, 0, XROWS * hd:XROWS * (hd + 1)])

    def score(hd, j, rows, slot=cur):
        u = hd // 2
        kblk = kbuf_ref[pl.ds(pl.multiple_of(j * TK, TK), rows), 256 * u:256 * (u + 1)]
        return jnp.dot(kblk, qpad_ref[slot, hd], preferred_element_type=jnp.float32)

    _first_step_prime(i == 0, st_ref, fill_queries, (q_ref, xq_ref), score)
    fill_queries(nxt, (qnext_ref, xqnext_ref))
    _flash_init(m_ref, acc_ref)

    def step(j, rows, kind, prefetch_j):
        if prefetch_j is None:
            prefetch = lambda hd: score(hd, 0, 2 * TK, nxt)
        else:
            prefetch = lambda hd: score(hd, prefetch_j, 2 * TK)

        def update(hd, st):
            if kind != "far":
                row = lax.broadcasted_iota(jnp.int32, st.shape, 0) - (rows - TK)
                col = lax.broadcasted_iota(jnp.int32, st.shape, 1)
                st = jnp.where(row > col, NEG, st)
            _flash_update(m_ref, acc_ref, hd, st, _value_rows(v_ref, j, rows, FV_ROWS * hd, FV_ROWS))

        done = (lambda hd: emit_pair(hd // 2) if hd % 2 == 1 else None) if prefetch_j is None else None
        _run_chains(FOX_HEADS, st_ref, score, update, j, rows, prefetch, done)

    def emit_pair(u):
        o = jnp.concatenate([acc_ref[hd, :d] * (1.0 / acc_ref[hd, d:d + 1]) for hd in (2 * u, 2 * u + 1)],
                            axis=0)
        o_ref[0, :, 128 * u:128 * (u + 1)] = o.T.astype(o_ref.dtype)

    _key_block_schedule(i, step)


def _fox_attention(kf, qf, xq, vf):
    B, S, _ = kf.shape
    nq = S // TQ
    return pl.pallas_call(
        _fox_kernel,
        grid=(B, nq),
        in_specs=[
            pl.BlockSpec((1, S, 2 * FOX_WIDTH), lambda b, i: (b, 0, 0)),
            pl.BlockSpec((1, 1, FOX_WIDTH, TQ), lambda b, i: (b, i, 0, 0)),
            pl.BlockSpec((1, 1, FOX_HEADS * XROWS, TQ), lambda b, i: (b, i, 0, 0)),
            pl.BlockSpec((1, 1, FOX_WIDTH, TQ), lambda b, i: (b, jnp.minimum(i + 1, nq - 1), 0, 0)),
            pl.BlockSpec((1, 1, FOX_HEADS * XROWS, TQ), lambda b, i: (b, jnp.minimum(i + 1, nq - 1), 0, 0)),
            pl.BlockSpec((1, nq, FOX_HEADS * FV_ROWS, TK), lambda b, i: (b, 0, 0, 0)),
        ],
        out_specs=pl.BlockSpec((1, TQ, FOX_WIDTH), lambda b, i: (b, i, 0)),
        out_shape=jax.ShapeDtypeStruct((B, S, FOX_WIDTH), jnp.bfloat16),
        scratch_shapes=[
            pltpu.VMEM((2, FOX_HEADS, 256, TQ), jnp.bfloat16),
            pltpu.VMEM((FOX_HEADS, 1, TQ), jnp.float32),
            pltpu.VMEM((FOX_HEADS, FV_ROWS, TQ), jnp.float32),
            pltpu.VMEM((LOOKAHEAD, 2 * TK, TQ), jnp.float32),
            pltpu.VMEM((S, 2 * FOX_WIDTH), jnp.bfloat16),
        ],
        compiler_params=pltpu.CompilerParams(
            dimension_semantics=("arbitrary", "arbitrary"), vmem_limit_bytes=VMEM_LIMIT),
        name="fox_attention",
    )(kf, qf, xq, qf, xq, vf)


def _post_kernel(md_ref, mf_ref, x_ref, wod_ref, wof_ref, wup_ref, wdn_ref, g1_ref, g2_ref, g3_ref,
                 o_ref):
    subs = [slice(s * POST_SUB, (s + 1) * POST_SUB) for s in range(x_ref.shape[0] // POST_SUB)]
    attn = [jnp.dot(md_ref[rows], wod_ref[...], preferred_element_type=jnp.float32)
            + jnp.dot(mf_ref[rows], wof_ref[...], preferred_element_type=jnp.float32) for rows in subs]
    for rows, a in zip(subs, attn):
        x1 = x_ref[rows] + _rms(a, g1_ref[...])
        h2 = _rms(x1, g2_ref[...]).astype(jnp.bfloat16)
        acc = None
        for c in range(D_FF // FF_CHUNK):
            u = jnp.dot(h2, wup_ref[:, c * FF_CHUNK:(c + 1) * FF_CHUNK], preferred_element_type=jnp.float32)
            u = jnp.square(jnp.maximum(u, 0.0)).astype(jnp.bfloat16)
            part = jnp.dot(u, wdn_ref[c * FF_CHUNK:(c + 1) * FF_CHUNK, :], preferred_element_type=jnp.float32)
            acc = part if acc is None else acc + part
        o_ref[rows] = x1 + _rms(acc, g3_ref[...])


def _post(mix_d, mix_f, x2d, wod, wof, wup, wdn, g1, g2, g3):
    R, D = x2d.shape
    tm = TM_POST
    resident = lambda shape: pl.BlockSpec(shape, lambda r: (0,) * len(shape),
                                          pipeline_mode=pl.Buffered(1))
    return pl.pallas_call(
        _post_kernel,
        grid=(R // tm,),
        in_specs=[
            pl.BlockSpec((tm, DIFF_WIDTH), lambda r: (r, 0)),
            pl.BlockSpec((tm, FOX_WIDTH), lambda r: (r, 0)),
            pl.BlockSpec((tm, D), lambda r: (r, 0)),
            resident(wod.shape), resident(wof.shape), resident(wup.shape), resident(wdn.shape),
            resident((1, D)), resident((1, D)), resident((1, D)),
        ],
        out_specs=pl.BlockSpec((tm, D), lambda r: (r, 0)),
        out_shape=jax.ShapeDtypeStruct((R, D), jnp.float32),
        compiler_params=pltpu.CompilerParams(
            dimension_semantics=("parallel",), vmem_limit_bytes=VMEM_LIMIT),
        name="post",
    )(mix_d, mix_f, x2d, wod, wof, wup, wdn, g1, g2, g3)


def kernel(x, ln_attn_pre, w_in, b_f, lam_q1, lam_k1, lam_q2, lam_k2, subln_g, rel_bias,
           w_out, ln_attn_post, ln_mlp_pre, w_up, w_down, ln_mlp_post):
    B, S, D = x.shape
    depth = w_in.shape[0]
    bf16 = jnp.bfloat16
    bias_tiles, far_bucket = _bias_tiles(rel_bias, S)
    dw, fw = DIFF_WIDTH, FOX_WIDTH
    for l in range(depth):
        lambda_init = 0.8 - 0.6 * math.exp(-0.3 * l)
        w = w_in[l]
        o_dq, o_dk, o_dv, o_fq, o_fk, o_fv, o_g = 0, dw, 2 * dw, 3 * dw, 3 * dw + fw, 3 * dw + 2 * fw, 3 * dw + 3 * fw
        wrow = jnp.concatenate([w[:, o_dk:o_dk + dw], w[:, o_fk:o_fk + fw]], axis=1).astype(bf16)
        wcol = jnp.concatenate([w[:, o_dq:o_dq + dw], w[:, o_dv:o_dv + dw],
                                w[:, o_fq:o_fq + fw], w[:, o_fv:o_fv + fw]], axis=1).T.astype(bf16)
        wf = jnp.zeros((XROWS, D), bf16).at[:FOX_HEADS].set(w[:, o_g:o_g + FOX_HEADS].T.astype(bf16))
        bf_col = jnp.zeros((XROWS, 1), jnp.float32).at[:FOX_HEADS, 0].set(b_f[l].astype(jnp.float32))

        kd, kf, qd, vd, qf, vf, xq = _proj(x, ln_attn_pre[l][None, :], wrow, wcol, wf, bf_col)
        mix_d = _diff_attention(kd, qd, vd, bias_tiles, far_bucket, rel_bias,
                                lam_q1[l][None, :], lam_k1[l][None, :], lam_q2[l][None, :],
                                lam_k2[l][None, :], subln_g[l][None, :], lambda_init)
        mix_f = _fox_attention(kf, qf, xq, vf)
        wo = w_out[l].astype(bf16)
        y = _post(mix_d.reshape(B * S, dw), mix_f.reshape(B * S, fw), x.reshape(B * S, D),
                  wo[:dw], wo[dw:], w_up[l].astype(bf16), w_down[l].astype(bf16),
                  ln_attn_post[l][None, :], ln_mlp_pre[l][None, :], ln_mlp_post[l][None, :])
        x = y.reshape(B, S, D)
    return x
```

```python
import functools
import math

import numpy as np
import jax
import jax.numpy as jnp
from jax import lax
from jax.experimental import pallas as pl
from jax.experimental.pallas import tpu as pltpu

D_MODEL = 1024
DIFF_HEADS = 4
DIFF_HEAD_DIM = 64
DIFF_WIDTH = DIFF_HEADS * 2 * DIFF_HEAD_DIM
FOX_HEADS = 8
FOX_HEAD_DIM = 64
FOX_WIDTH = FOX_HEADS * FOX_HEAD_DIM
FOX_UNITS = FOX_HEADS // 2
D_FF = 4 * D_MODEL
N_BUCKETS = 32
MAX_DISTANCE = 128
EPS = 1e-6
NEG = -1e30
QK_SCALE = DIFF_HEAD_DIM ** -0.5
LOG2E = math.log2(math.e)

TQ = 256
TK = 256
TM_PROJ = 1024
TM_POST = 1024
POST_SUB = 512
FF_CHUNK = 1024
LOOKAHEAD = 8
ONES_ROWS = 16
DV_ROWS = 2 * DIFF_HEAD_DIM + ONES_ROWS
FV_ROWS = FOX_HEAD_DIM + ONES_ROWS
XROWS = 16
VMEM_LIMIT = 56 * 1024 * 1024


def _rms(x, g):
    ms = jnp.mean(x * x, axis=-1, keepdims=True)
    return x * lax.rsqrt(ms + EPS) * g


def _proj_kernel(x_ref, g_ref, wrow_ref, wcol_ref, wf_ref, bf_ref, pq_ref, pk_ref, oq_ref, ok_ref,
                 kd_ref, kf_ref, qd_ref, vd_ref, qf_ref, vf_ref, xq_ref, carry_ref):
    tm = x_ref.shape[1]
    nsub = tm // TQ
    nt = (((1,), (1,)), ((), ()))
    ones = jnp.ones((ONES_ROWS, TQ), jnp.bfloat16)
    outs = ((qd_ref, None, None), (vd_ref, 2 * DIFF_HEAD_DIM, DV_ROWS),
            (qf_ref, None, None), (vf_ref, FOX_HEAD_DIM, FV_ROWS))

    def column_outputs(c, h, n):
        o_ref, hrows, orows = outs[n]
        colt = lax.dot_general(wcol_ref[n * 512:(n + 1) * 512, :], h, nt,
                               preferred_element_type=jnp.float32)
        if hrows is None:
            o_ref[0, c] = (colt * (QK_SCALE * LOG2E)).astype(jnp.bfloat16)
        else:
            colt = colt.astype(jnp.bfloat16)
            for hd in range(512 // hrows):
                o_ref[0, c, orows * hd:orows * hd + hrows] = colt[hrows * hd:hrows * (hd + 1)]
                o_ref[0, c, orows * hd + hrows:orows * (hd + 1)] = ones

    ft_parts = []
    for c in range(nsub):
        rows = slice(c * TQ, (c + 1) * TQ)
        h = _rms(x_ref[0, rows], g_ref[...]).astype(jnp.bfloat16)
        ft_parts.append(lax.dot_general(wf_ref[...], h, nt, preferred_element_type=jnp.float32))
        krow = jnp.dot(h, wrow_ref[...], preferred_element_type=jnp.float32)
        kd_ref[0, rows] = krow[:, :DIFF_WIDTH].astype(jnp.bfloat16)
        for u in range(FOX_UNITS):
            kf_ref[0, rows, 256 * u:256 * u + 128] = (
                krow[:, DIFF_WIDTH + 128 * u:DIFF_WIDTH + 128 * (u + 1)].astype(jnp.bfloat16))
        for n in range(len(outs)):
            column_outputs(c, h, n)

    ft = jnp.concatenate(ft_parts, axis=1) + bf_ref[...]
    ls = jnp.minimum(ft, 0.0) - jnp.log1p(jnp.exp(-jnp.abs(ft)))
    lane = lax.broadcasted_iota(jnp.int32, ls.shape, 1)
    sh = 1
    while sh < tm:
        ls = ls + jnp.where(lane >= sh, pltpu.roll(ls, sh, axis=1), 0.0)
        sh *= 2

    @pl.when(pl.program_id(1) == 0)
    def _():
        carry_ref[...] = jnp.zeros_like(carry_ref)

    cum = ls + carry_ref[...]
    carry_ref[...] = cum[:, tm - 1:tm]
    cum = cum * LOG2E

    hi = cum.astype(jnp.bfloat16)
    r1 = cum - hi.astype(jnp.float32)
    mid = r1.astype(jnp.bfloat16)
    lo = (r1 - mid.astype(jnp.float32)).astype(jnp.bfloat16)
    parts = jnp.concatenate([hi, mid, lo, jnp.zeros((128 - 3 * XROWS, tm), jnp.bfloat16)], axis=0)
    xq = jnp.dot(pq_ref[...], parts, preferred_element_type=jnp.float32) + oq_ref[...]
    xq = xq.astype(jnp.bfloat16)
    for c in range(nsub):
        xq_ref[0, c] = xq[:, c * TQ:(c + 1) * TQ]
    xk = jnp.dot(pk_ref[...], parts, preferred_element_type=jnp.float32) + ok_ref[...]
    for u in range(FOX_UNITS):
        kf_ref[0, :, 256 * u + 128:256 * (u + 1)] = (
            xk[128 * u:128 * (u + 1), :].T.astype(jnp.bfloat16))


def _placement_constants():
    pq = np.zeros((FOX_UNITS * 2 * XROWS, 128), np.float32)
    oq = np.zeros((FOX_UNITS * 2 * XROWS, 1), np.float32)
    pk = np.zeros((FOX_UNITS * 128, 128), np.float32)
    ok = np.zeros((FOX_UNITS * 128, 1), np.float32)
    for u in range(FOX_UNITS):
        for a in range(2):
            head = 2 * u + a
            for r in range(3):
                pq[2 * XROWS * u + XROWS * a + r, XROWS * r + head] = 1.0
                oq[2 * XROWS * u + XROWS * a + 3 + r, 0] = 1.0
                pk[128 * u + XROWS * a + 3 + r, XROWS * r + head] = -1.0
                ok[128 * u + XROWS * a + r, 0] = 1.0
    return (jnp.asarray(pq, jnp.bfloat16), jnp.asarray(pk, jnp.bfloat16),
            jnp.asarray(oq), jnp.asarray(ok))


def _proj(x, g, wrow, wcol, wf, bf_col):
    B, S, D = x.shape
    tm = TM_PROJ
    nsub = tm // TQ
    nq = S // TQ
    pq, pk, oq, ok = _placement_constants()
    const = lambda shape: pl.BlockSpec(shape, lambda b, t: (0,) * len(shape))
    colspec = lambda rows: pl.BlockSpec((1, nsub, rows, TQ), lambda b, t: (b, t, 0, 0))
    bf16 = jnp.bfloat16
    return pl.pallas_call(
        _proj_kernel,
        grid=(B, S // tm),
        in_specs=[
            pl.BlockSpec((1, tm, D), lambda b, t: (b, t, 0)),
            const((1, D)), const(wrow.shape), const(wcol.shape), const(wf.shape), const(bf_col.shape),
            const(pq.shape), const(pk.shape), const(oq.shape), const(ok.shape),
        ],
        out_specs=[
            pl.BlockSpec((1, tm, DIFF_WIDTH), lambda b, t: (b, t, 0)),
            pl.BlockSpec((1, tm, 2 * FOX_WIDTH), lambda b, t: (b, t, 0)),
            colspec(512), colspec(DIFF_HEADS * DV_ROWS), colspec(512), colspec(FOX_HEADS * FV_ROWS),
            colspec(FOX_UNITS * 2 * XROWS),
        ],
        out_shape=[
            jax.ShapeDtypeStruct((B, S, DIFF_WIDTH), bf16),
            jax.ShapeDtypeStruct((B, S, 2 * FOX_WIDTH), bf16),
            jax.ShapeDtypeStruct((B, nq, 512, TQ), bf16),
            jax.ShapeDtypeStruct((B, nq, DIFF_HEADS * DV_ROWS, TQ), bf16),
            jax.ShapeDtypeStruct((B, nq, 512, TQ), bf16),
            jax.ShapeDtypeStruct((B, nq, FOX_HEADS * FV_ROWS, TQ), bf16),
            jax.ShapeDtypeStruct((B, nq, FOX_UNITS * 2 * XROWS, TQ), bf16),
        ],
        scratch_shapes=[pltpu.VMEM((XROWS, 1), jnp.float32)],
        compiler_params=pltpu.CompilerParams(
            dimension_semantics=("arbitrary", "arbitrary"), vmem_limit_bytes=VMEM_LIMIT),
        name="proj",
    )(x, g, wrow, wcol, wf, bf_col, pq, pk, oq, ok)


def _t5_buckets(n):
    max_exact = N_BUCKETS // 2
    dist = np.arange(n)
    d = np.maximum(dist, 1).astype(np.float32)
    large = np.float32(max_exact) + (np.log(d / np.float32(max_exact))
                                     / np.float32(math.log(MAX_DISTANCE / max_exact))
                                     * np.float32(N_BUCKETS - max_exact))
    large = np.minimum(large.astype(np.int32), N_BUCKETS - 1)
    return np.where(dist < max_exact, dist, large).astype(np.int32)


def _bias_kernel(rel_ref, idx_ref, out_ref):
    hd = pl.program_id(0)
    idx = idx_ref[...]
    val = jnp.full(idx.shape, NEG, jnp.float32)
    for bkt in range(N_BUCKETS):
        val = jnp.where(idx == bkt, rel_ref[bkt, hd] * LOG2E, val)
    out_ref[0] = val


def _bias_tiles(rel_bias, S):
    buckets = _t5_buckets(S)
    far = int(buckets[-1])
    assert np.all(buckets[TQ + 1:] == far)
    s_loc = np.arange(TK)[:, None]
    t_loc = np.arange(TQ)[None, :]
    rel0 = t_loc - s_loc
    idx0 = np.where(rel0 >= 0, buckets[np.maximum(rel0, 0)], -1)
    idx1 = buckets[TQ + t_loc - s_loc]
    idx = jnp.asarray(np.concatenate([idx1, idx0], axis=0).astype(np.int32))
    tiles = pl.pallas_call(
        _bias_kernel,
        grid=(DIFF_HEADS,),
        in_specs=[pl.BlockSpec(memory_space=pltpu.SMEM),
                  pl.BlockSpec((2 * TK, TQ), lambda h: (0, 0))],
        out_specs=pl.BlockSpec((1, 2 * TK, TQ), lambda h: (h, 0, 0)),
        out_shape=jax.ShapeDtypeStruct((DIFF_HEADS, 2 * TK, TQ), jnp.float32),
        name="t5_bias_tiles",
    )(rel_bias, idx)
    return tiles, far


def _flash_update(m_ref, acc_ref, a, st, vt, shift=None):
    m_old = m_ref[a]
    mx = jnp.max(st, axis=0, keepdims=True)
    if shift is None:
        m_new = jnp.maximum(m_old, mx)
        x = st - m_new
    else:
        m_new = jnp.maximum(m_old, mx + shift)
        x = st - (m_new - shift)
    p = jnp.exp2(x.astype(jnp.bfloat16))
    alpha = jnp.exp2(m_old - m_new)
    acc_ref[a] = alpha * acc_ref[a] + jnp.dot(vt, p, preferred_element_type=jnp.float32)
    m_ref[a] = m_new


def _flash_init(m_ref, acc_ref):
    m_ref[...] = jnp.full(m_ref.shape, NEG, jnp.float32)
    acc_ref[...] = jnp.zeros_like(acc_ref)


def _run_chains(n, st_ref, score_fn, update_fn, j, rows, prefetch_fn, done_fn=None):
    def finish(item):
        update_fn(*item)
        if done_fn is not None:
            done_fn(item[0])

    def staged(c):
        if rows <= 2 * TK:
            return st_ref[c, :rows]
        return jnp.concatenate([st_ref[c], score_fn(c, j + 2, rows - 2 * TK)], axis=0)

    pending = [(c, staged(c)) for c in range(LOOKAHEAD)]
    for c in range(LOOKAHEAD, n):
        pending.append((c, score_fn(c, j, rows)))
        finish(pending.pop(0))
    st_ref[0] = prefetch_fn(0)
    for c in range(LOOKAHEAD):
        if c + 1 < LOOKAHEAD:
            st_ref[c + 1] = prefetch_fn(c + 1)
        finish(pending.pop(0))


def _value_rows(v_ref, j, rows, r0, nrows):
    blocks = [v_ref[0, j + b, r0:r0 + nrows] for b in range(rows // TK)]
    return blocks[0] if len(blocks) == 1 else jnp.concatenate(blocks, axis=1)


def _stage_keys(k_ref, kbuf_ref, qpad_ref, first):
    @pl.when(first)
    def _():
        def body(t, carry):
            r = pl.multiple_of(t * TK, TK)
            kbuf_ref[pl.ds(r, TK), :] = k_ref[0, pl.ds(r, TK), :]
            return carry

        lax.fori_loop(0, k_ref.shape[1] // TK, body, 0)
        qpad_ref[...] = jnp.zeros_like(qpad_ref)


def _first_step_prime(first, st_ref, fill_queries, q_ref, score_fn):
    @pl.when(first)
    def _():
        fill_queries(0, q_ref)
        for c in range(LOOKAHEAD):
            st_ref[c] = score_fn(c, 0, 2 * TK, 0)


def _key_block_schedule(i, step):
    nfar = jnp.maximum(i - 1, 0)
    odd = (nfar & 1) == 1

    def far_pair(jj, carry):
        step(2 * jj, 2 * TK, "far", 2 * jj + 2)
        return carry

    lax.fori_loop(0, lax.shift_right_logical(nfar, 1), far_pair, 0)

    @pl.when(odd)
    def _():
        step(i - 2, 3 * TK, "last", None)

    @pl.when(jnp.logical_and(i >= 1, jnp.logical_not(odd)))
    def _():
        step(i - 1, 2 * TK, "last", None)

    @pl.when(i == 0)
    def _():
        step(0, TK, "last", None)


def _diff_kernel(far_bucket, lambda_init,
                 rel_ref, k_ref, q_ref, qnext_ref, v_ref, bias_ref, lq1_ref, lk1_ref, lq2_ref, lk2_ref,
                 g_ref, o_ref, qpad_ref, m_ref, acc_ref, st_ref, kbuf_ref):
    i = pl.program_id(1)
    d = DIFF_HEAD_DIM
    cur, nxt = i & 1, (i + 1) & 1
    _stage_keys(k_ref, kbuf_ref, qpad_ref, i == 0)

    def fill_queries(slot, src_ref):
        for c in range(2 * DIFF_HEADS):
            r0 = d * (c % 4)
            qpad_ref[slot, c, r0:r0 + d] = src_ref[0, 0, d * c:d * (c + 1)]

    def score(c, j, rows, slot=cur):
        pair = c // 4
        kblk = kbuf_ref[pl.ds(pl.multiple_of(j * TK, TK), rows), 256 * pair:256 * (pair + 1)]
        return jnp.dot(kblk, qpad_ref[slot, c], preferred_element_type=jnp.float32)

    _first_step_prime(i == 0, st_ref, fill_queries, q_ref, score)
    fill_queries(nxt, qnext_ref)
    _flash_init(m_ref, acc_ref)

    def step(j, rows, kind, prefetch_j):
        if prefetch_j is None:
            prefetch = lambda c: score(c, 0, 2 * TK, nxt)
        else:
            prefetch = lambda c: score(c, prefetch_j, 2 * TK)

        def update(c, st):
            hd = c // 2
            vt = _value_rows(v_ref, j, rows, DV_ROWS * hd, DV_ROWS)
            far_bias = rel_ref[far_bucket, hd] * LOG2E
            if kind == "far":
                _flash_update(m_ref, acc_ref, c, st, vt, shift=far_bias)
            else:
                bias = bias_ref[hd, 2 * TK - min(rows, 2 * TK):]
                if rows > 2 * TK:
                    bias = jnp.concatenate(
                        [jnp.full((rows - 2 * TK, TQ), far_bias, jnp.float32), bias], axis=0)
                _flash_update(m_ref, acc_ref, c, bias + st, vt)

        done = (lambda c: emit_head(c // 2) if c % 2 == 1 else None) if prefetch_j is None else None
        _run_chains(2 * DIFF_HEADS, st_ref, score, update, j, rows, prefetch, done)

    def emit_head(hd):
        lam = (jnp.exp(jnp.sum(lq1_ref[...] * lk1_ref[...], axis=-1, keepdims=True))
               - jnp.exp(jnp.sum(lq2_ref[...] * lk2_ref[...], axis=-1, keepdims=True))
               + lambda_init)
        dv = 2 * d
        o0 = acc_ref[2 * hd, :dv] * (1.0 / acc_ref[2 * hd, dv:dv + 1])
        o1 = acc_ref[2 * hd + 1, :dv] * (1.0 / acc_ref[2 * hd + 1, dv:dv + 1])
        od = o0 - lam * o1
        ms = jnp.mean(od * od, axis=0, keepdims=True)
        y = (od * lax.rsqrt(ms + EPS)).T * g_ref[...]
        o_ref[0, :, 128 * hd:128 * (hd + 1)] = (y * (1.0 - lambda_init)).astype(o_ref.dtype)

    _key_block_schedule(i, step)


def _diff_attention(kd, qd, vd, bias_tiles, far_bucket, rel_bias, lq1, lk1, lq2, lk2, g, lambda_init):
    B, S, _ = kd.shape
    nq = S // TQ
    nmaps = 2 * DIFF_HEADS
    vec = lambda n: pl.BlockSpec((1, n), lambda b, i: (0, 0))
    return pl.pallas_call(
        functools.partial(_diff_kernel, far_bucket, lambda_init),
        grid=(B, nq),
        in_specs=[
            pl.BlockSpec(memory_space=pltpu.SMEM),
            pl.BlockSpec((1, S, DIFF_WIDTH), lambda b, i: (b, 0, 0)),
            pl.BlockSpec((1, 1, DIFF_WIDTH, TQ), lambda b, i: (b, i, 0, 0)),
            pl.BlockSpec((1, 1, DIFF_WIDTH, TQ), lambda b, i: (b, jnp.minimum(i + 1, nq - 1), 0, 0)),
            pl.BlockSpec((1, nq, DIFF_HEADS * DV_ROWS, TK), lambda b, i: (b, 0, 0, 0)),
            pl.BlockSpec((DIFF_HEADS, 2 * TK, TQ), lambda b, i: (0, 0, 0)),
            vec(DIFF_HEAD_DIM), vec(DIFF_HEAD_DIM), vec(DIFF_HEAD_DIM), vec(DIFF_HEAD_DIM),
            vec(2 * DIFF_HEAD_DIM),
        ],
        out_specs=pl.BlockSpec((1, TQ, DIFF_WIDTH), lambda b, i: (b, i, 0)),
        out_shape=jax.ShapeDtypeStruct((B, S, DIFF_WIDTH), jnp.bfloat16),
        scratch_shapes=[
            pltpu.VMEM((2, nmaps, 256, TQ), jnp.bfloat16),
            pltpu.VMEM((nmaps, 1, TQ), jnp.float32),
            pltpu.VMEM((nmaps, DV_ROWS, TQ), jnp.float32),
            pltpu.VMEM((LOOKAHEAD, 2 * TK, TQ), jnp.float32),
            pltpu.VMEM((S, DIFF_WIDTH), jnp.bfloat16),
        ],
        compiler_params=pltpu.CompilerParams(
            dimension_semantics=("arbitrary", "arbitrary"), vmem_limit_bytes=VMEM_LIMIT),
        name="diff_attention",
    )(rel_bias, kd, qd, qd, vd, bias_tiles, lq1, lk1, lq2, lk2, g)


def _fox_kernel(k_ref, q_ref, xq_ref, qnext_ref, xqnext_ref, v_ref, o_ref,
                qpad_ref, m_ref, acc_ref, st_ref, kbuf_ref):
    i = pl.program_id(1)
    d = FOX_HEAD_DIM
    cur, nxt = i & 1, (i + 1) & 1
    _stage_keys(k_ref, kbuf_ref, qpad_ref, i == 0)

    def fill_queries(slot, src_refs):
        qsrc_ref, xsrc_ref = src_refs
        for hd in range(FOX_HEADS):
            a = hd % 2
            qpad_ref[slot, hd, d * a:d * (a + 1)] = qsrc_ref[0, 0, d * hd:d * (hd + 1)]
            qpad_ref[slot, hd, 128 + XROWS * a:128 + XROWS * (a + 1)] = (
                xsrc_ref[0, 0, XROWS * hd:XROWS * (hd + 1)])

    def score(hd, j, rows, slot=cur):
        u = hd // 2
        kblk = kbuf_ref[pl.ds(pl.multiple_of(j * TK, TK), rows), 256 * u:256 * (u + 1)]
        return jnp.dot(kblk, qpad_ref[slot, hd], preferred_element_type=jnp.float32)

    _first_step_prime(i == 0, st_ref, fill_queries, (q_ref, xq_ref), score)
    fill_queries(nxt, (qnext_ref, xqnext_ref))
    _flash_init(m_ref, acc_ref)

    def step(j, rows, kind, prefetch_j):
        if prefetch_j is None:
            prefetch = lambda hd: score(hd, 0, 2 * TK, nxt)
        else:
            prefetch = lambda hd: score(hd, prefetch_j, 2 * TK)

        def update(hd, st):
            if kind != "far":
                row = lax.broadcasted_iota(jnp.int32, st.shape, 0) - (rows - TK)
                col = lax.broadcasted_iota(jnp.int32, st.shape, 1)
                st = jnp.where(row > col, NEG, st)
            _flash_update(m_ref, acc_ref, hd, st, _value_rows(v_ref, j, rows, FV_ROWS * hd, FV_ROWS))

        done = (lambda hd: emit_pair(hd // 2) if hd % 2 == 1 else None) if prefetch_j is None else None
        _run_chains(FOX_HEADS, st_ref, score, update, j, rows, prefetch, done)

    def emit_pair(u):
        o = jnp.concatenate([acc_ref[hd, :d] * (1.0 / acc_ref[hd, d:d + 1]) for hd in (2 * u, 2 * u + 1)],
                            axis=0)
        o_ref[0, :, 128 * u:128 * (u + 1)] = o.T.astype(o_ref.dtype)

    _key_block_schedule(i, step)


def _fox_attention(kf, qf, xq, vf):
    B, S, _ = kf.shape
    nq = S // TQ
    return pl.pallas_call(
        _fox_kernel,
        grid=(B, nq),
        in_specs=[
            pl.BlockSpec((1, S, 2 * FOX_WIDTH), lambda b, i: (b, 0, 0)),
            pl.BlockSpec((1, 1, FOX_WIDTH, TQ), lambda b, i: (b, i, 0, 0)),
            pl.BlockSpec((1, 1, FOX_HEADS * XROWS, TQ), lambda b, i: (b, i, 0, 0)),
            pl.BlockSpec((1, 1, FOX_WIDTH, TQ), lambda b, i: (b, jnp.minimum(i + 1, nq - 1), 0, 0)),
            pl.BlockSpec((1, 1, FOX_HEADS * XROWS, TQ), lambda b, i: (b, jnp.minimum(i + 1, nq - 1), 0, 0)),
            pl.BlockSpec((1, nq, FOX_HEADS * FV_ROWS, TK), lambda b, i: (b, 0, 0, 0)),
        ],
        out_specs=pl.BlockSpec((1, TQ, FOX_WIDTH), lambda b, i: (b, i, 0)),
        out_shape=jax.ShapeDtypeStruct((B, S, FOX_WIDTH), jnp.bfloat16),
        scratch_shapes=[
            pltpu.VMEM((2, FOX_HEADS, 256, TQ), jnp.bfloat16),
            pltpu.VMEM((FOX_HEADS, 1, TQ), jnp.float32),
            pltpu.VMEM((FOX_HEADS, FV_ROWS, TQ), jnp.float32),
            pltpu.VMEM((LOOKAHEAD, 2 * TK, TQ), jnp.float32),
            pltpu.VMEM((S, 2 * FOX_WIDTH), jnp.bfloat16),
        ],
        compiler_params=pltpu.CompilerParams(
            dimension_semantics=("arbitrary", "arbitrary"), vmem_limit_bytes=VMEM_LIMIT),
        name="fox_attention",
    )(kf, qf, xq, qf, xq, vf)


def _post_kernel(md_ref, mf_ref, x_ref, wod_ref, wof_ref, wup_ref, wdn_ref, g1_ref, g2_ref, g3_ref,
                 o_ref):
    subs = [slice(s * POST_SUB, (s + 1) * POST_SUB) for s in range(x_ref.shape[0] // POST_SUB)]
    attn = [jnp.dot(md_ref[rows], wod_ref[...], preferred_element_type=jnp.float32)
            + jnp.dot(mf_ref[rows], wof_ref[...], preferred_element_type=jnp.float32) for rows in subs]
    for rows, a in zip(subs, attn):
        x1 = x_ref[rows] + _rms(a, g1_ref[...])
        h2 = _rms(x1, g2_ref[...]).astype(jnp.bfloat16)
        acc = None
        for c in range(D_FF // FF_CHUNK):
            u = jnp.dot(h2, wup_ref[:, c * FF_CHUNK:(c + 1) * FF_CHUNK], preferred_element_type=jnp.float32)
            u = jnp.square(jnp.maximum(u, 0.0)).astype(jnp.bfloat16)
            part = jnp.dot(u, wdn_ref[c * FF_CHUNK:(c + 1) * FF_CHUNK, :], preferred_element_type=jnp.float32)
            acc = part if acc is None else acc + part
        o_ref[rows] = x1 + _rms(acc, g3_ref[...])


def _post(mix_d, mix_f, x2d, wod, wof, wup, wdn, g1, g2, g3):
    R, D = x2d.shape
    tm = TM_POST
    resident = lambda shape: pl.BlockSpec(shape, lambda r: (0,) * len(shape),
                                          pipeline_mode=pl.Buffered(1))
    return pl.pallas_call(
        _post_kernel,
        grid=(R // tm,),
        in_specs=[
            pl.BlockSpec((tm, DIFF_WIDTH), lambda r: (r, 0)),
            pl.BlockSpec((tm, FOX_WIDTH), lambda r: (r, 0)),
            pl.BlockSpec((tm, D), lambda r: (r, 0)),
            resident(wod.shape), resident(wof.shape), resident(wup.shape), resident(wdn.shape),
            resident((1, D)), resident((1, D)), resident((1, D)),
        ],
        out_specs=pl.BlockSpec((tm, D), lambda r: (r, 0)),
        out_shape=jax.ShapeDtypeStruct((R, D), jnp.float32),
        compiler_params=pltpu.CompilerParams(
            dimension_semantics=("parallel",), vmem_limit_bytes=VMEM_LIMIT),
        name="post",
    )(mix_d, mix_f, x2d, wod, wof, wup, wdn, g1, g2, g3)


def kernel(x, ln_attn_pre, w_in, b_f, lam_q1, lam_k1, lam_q2, lam_k2, subln_g, rel_bias,
           w_out, ln_attn_post, ln_mlp_pre, w_up, w_down, ln_mlp_post):
    B, S, D = x.shape
    depth = w_in.shape[0]
    bf16 = jnp.bfloat16
    bias_tiles, far_bucket = _bias_tiles(rel_bias, S)
    dw, fw = DIFF_WIDTH, FOX_WIDTH
    for l in range(depth):
        lambda_init = 0.8 - 0.6 * math.exp(-0.3 * l)
        w = w_in[l]
        o_dq, o_dk, o_dv, o_fq, o_fk, o_fv, o_g = 0, dw, 2 * dw, 3 * dw, 3 * dw + fw, 3 * dw + 2 * fw, 3 * dw + 3 * fw
        wrow = jnp.concatenate([w[:, o_dk:o_dk + dw], w[:, o_fk:o_fk + fw]], axis=1).astype(bf16)
        wcol = jnp.concatenate([w[:, o_dq:o_dq + dw], w[:, o_dv:o_dv + dw],
                                w[:, o_fq:o_fq + fw], w[:, o_fv:o_fv + fw]], axis=1).T.astype(bf16)
        wf = jnp.zeros((XROWS, D), bf16).at[:FOX_HEADS].set(w[:, o_g:o_g + FOX_HEADS].T.astype(bf16))
        bf_col = jnp.zeros((XROWS, 1), jnp.float32).at[:FOX_HEADS, 0].set(b_f[l].astype(jnp.float32))

        kd, kf, qd, vd, qf, vf, xq = _proj(x, ln_attn_pre[l][None, :], wrow, wcol, wf, bf_col)
        mix_d = _diff_attention(kd, qd, vd, bias_tiles, far_bucket, rel_bias,
                                lam_q1[l][None, :], lam_k1[l][None, :], lam_q2[l][None, :],
                                lam_k2[l][None, :], subln_g[l][None, :], lambda_init)
        mix_f = _fox_attention(kf, qf, xq, vf)
        wo = w_out[l].astype(bf16)
        y = _post(mix_d.reshape(B * S, dw), mix_f.reshape(B * S, fw), x.reshape(B * S, D),
                  wo[:dw], wo[dw:], w_up[l].astype(bf16), w_down[l].astype(bf16),
                  ln_attn_post[l][None, :], ln_mlp_pre[l][None, :], ln_mlp_post[l][None, :])
        x = y.reshape(B, S, D)
    return x
```

```python
import functools
import math

import numpy as np
import jax
import jax.numpy as jnp
from jax import lax
from jax.experimental import pallas as pl
from jax.experimental.pallas import tpu as pltpu

D_MODEL = 1024
DIFF_HEADS = 4
DIFF_HEAD_DIM = 64
DIFF_WIDTH = DIFF_HEADS * 2 * DIFF_HEAD_DIM
FOX_HEADS = 8
FOX_HEAD_DIM = 64
FOX_WIDTH = FOX_HEADS * FOX_HEAD_DIM
FOX_UNITS = FOX_HEADS // 2
D_FF = 4 * D_MODEL
N_BUCKETS = 32
MAX_DISTANCE = 128
EPS = 1e-6
NEG = -1e30
QK_SCALE = DIFF_HEAD_DIM ** -0.5
LOG2E = math.log2(math.e)

TQ = 256
TK = 256
TM_PROJ = 1024
TM_POST = 1024
POST_SUB = 512
FF_CHUNK = 1024
LOOKAHEAD = 8
ONES_ROWS = 16
DV_ROWS = 2 * DIFF_HEAD_DIM + ONES_ROWS
FV_ROWS = FOX_HEAD_DIM + ONES_ROWS
XROWS = 16
VMEM_LIMIT = 56 * 1024 * 1024


def _rms(x, g):
    ms = jnp.mean(x * x, axis=-1, keepdims=True)
    return x * lax.rsqrt(ms + EPS) * g


def _proj_kernel(x_ref, g_ref, wrow_ref, wcol_ref, wf_ref, bf_ref, pq_ref, pk_ref, oq_ref, ok_ref,
                 kd_ref, kf_ref, qd_ref, vd_ref, qf_ref, vf_ref, xq_ref, carry_ref):
    tm = x_ref.shape[1]
    nsub = tm // TQ
    nt = (((1,), (1,)), ((), ()))
    ones = jnp.ones((ONES_ROWS, TQ), jnp.bfloat16)
    outs = ((qd_ref, None, None), (vd_ref, 2 * DIFF_HEAD_DIM, DV_ROWS),
            (qf_ref, None, None), (vf_ref, FOX_HEAD_DIM, FV_ROWS))

    def column_outputs(c, h, n):
        o_ref, hrows, orows = outs[n]
        colt = lax.dot_general(wcol_ref[n * 512:(n + 1) * 512, :], h, nt,
                               preferred_element_type=jnp.float32)
        if hrows is None:
            o_ref[0, c] = (colt * (QK_SCALE * LOG2E)).astype(jnp.bfloat16)
        else:
            colt = colt.astype(jnp.bfloat16)
            for hd in range(512 // hrows):
                o_ref[0, c, orows * hd:orows * hd + hrows] = colt[hrows * hd:hrows * (hd + 1)]
                o_ref[0, c, orows * hd + hrows:orows * (hd + 1)] = ones

    ft_parts = []
    for c in range(nsub):
        rows = slice(c * TQ, (c + 1) * TQ)
        h = _rms(x_ref[0, rows], g_ref[...]).astype(jnp.bfloat16)
        ft_parts.append(lax.dot_general(wf_ref[...], h, nt, preferred_element_type=jnp.float32))
        krow = jnp.dot(h, wrow_ref[...], preferred_element_type=jnp.float32)
        kd_ref[0, rows] = krow[:, :DIFF_WIDTH].astype(jnp.bfloat16)
        for u in range(FOX_UNITS):
            kf_ref[0, rows, 256 * u:256 * u + 128] = (
                krow[:, DIFF_WIDTH + 128 * u:DIFF_WIDTH + 128 * (u + 1)].astype(jnp.bfloat16))
        for n in range(len(outs)):
            column_outputs(c, h, n)

    ft = jnp.concatenate(ft_parts, axis=1) + bf_ref[...]
    ls = jnp.minimum(ft, 0.0) - jnp.log1p(jnp.exp(-jnp.abs(ft)))
    lane = lax.broadcasted_iota(jnp.int32, ls.shape, 1)
    sh = 1
    while sh < tm:
        ls = ls + jnp.where(lane >= sh, pltpu.roll(ls, sh, axis=1), 0.0)
        sh *= 2

    @pl.when(pl.program_id(1) == 0)
    def _():
        carry_ref[...] = jnp.zeros_like(carry_ref)

    cum = ls + carry_ref[...]
    carry_ref[...] = cum[:, tm - 1:tm]
    cum = cum * LOG2E

    hi = cum.astype(jnp.bfloat16)
    r1 = cum - hi.astype(jnp.float32)
    mid = r1.astype(jnp.bfloat16)
    lo = (r1 - mid.astype(jnp.float32)).astype(jnp.bfloat16)
    parts = jnp.concatenate([hi, mid, lo, jnp.zeros((128 - 3 * XROWS, tm), jnp.bfloat16)], axis=0)
    xq = jnp.dot(pq_ref[...], parts, preferred_element_type=jnp.float32) + oq_ref[...]
    xq = xq.astype(jnp.bfloat16)
    for c in range(nsub):
        xq_ref[0, c] = xq[:, c * TQ:(c + 1) * TQ]
    xk = jnp.dot(pk_ref[...], parts, preferred_element_type=jnp.float32) + ok_ref[...]
    for u in range(FOX_UNITS):
        kf_ref[0, :, 256 * u + 128:256 * (u + 1)] = (
            xk[128 * u:128 * (u + 1), :].T.astype(jnp.bfloat16))


def _placement_constants():
    pq = np.zeros((FOX_UNITS * 2 * XROWS, 128), np.float32)
    oq = np.zeros((FOX_UNITS * 2 * XROWS, 1), np.float32)
    pk = np.zeros((FOX_UNITS * 128, 128), np.float32)
    ok = np.zeros((FOX_UNITS * 128, 1), np.float32)
    for u in range(FOX_UNITS):
        for a in range(2):
            head = 2 * u + a
            for r in range(3):
                pq[2 * XROWS * u + XROWS * a + r, XROWS * r + head] = 1.0
                oq[2 * XROWS * u + XROWS * a + 3 + r, 0] = 1.0
                pk[128 * u + XROWS * a + 3 + r, XROWS * r + head] = -1.0
                ok[128 * u + XROWS * a + r, 0] = 1.0
    return (jnp.asarray(pq, jnp.bfloat16), jnp.asarray(pk, jnp.bfloat16),
            jnp.asarray(oq), jnp.asarray(ok))


def _proj(x, g, wrow, wcol, wf, bf_col):
    B, S, D = x.shape
    tm = TM_PROJ
    nsub = tm // TQ
    nq = S // TQ
    pq, pk, oq, ok = _placement_constants()
    const = lambda shape: pl.BlockSpec(shape, lambda b, t: (0,) * len(shape))
    colspec = lambda rows: pl.BlockSpec((1, nsub, rows, TQ), lambda b, t: (b, t, 0, 0))
    bf16 = jnp.bfloat16
    return pl.pallas_call(
        _proj_kernel,
        grid=(B, S // tm),
        in_specs=[
            pl.BlockSpec((1, tm, D), lambda b, t: (b, t, 0)),
            const((1, D)), const(wrow.shape), const(wcol.shape), const(wf.shape), const(bf_col.shape),
            const(pq.shape), const(pk.shape), const(oq.shape), const(ok.shape),
        ],
        out_specs=[
            pl.BlockSpec((1, tm, DIFF_WIDTH), lambda b, t: (b, t, 0)),
            pl.BlockSpec((1, tm, 2 * FOX_WIDTH), lambda b, t: (b, t, 0)),
            colspec(512), colspec(DIFF_HEADS * DV_ROWS), colspec(512), colspec(FOX_HEADS * FV_ROWS),
            colspec(FOX_UNITS * 2 * XROWS),
        ],
        out_shape=[
            jax.ShapeDtypeStruct((B, S, DIFF_WIDTH), bf16),
            jax.ShapeDtypeStruct((B, S, 2 * FOX_WIDTH), bf16),
            jax.ShapeDtypeStruct((B, nq, 512, TQ), bf16),
            jax.ShapeDtypeStruct((B, nq, DIFF_HEADS * DV_ROWS, TQ), bf16),
            jax.ShapeDtypeStruct((B, nq, 512, TQ), bf16),
            jax.ShapeDtypeStruct((B, nq, FOX_HEADS * FV_ROWS, TQ), bf16),
            jax.ShapeDtypeStruct((B, nq, FOX_UNITS * 2 * XROWS, TQ), bf16),
        ],
        scratch_shapes=[pltpu.VMEM((XROWS, 1), jnp.float32)],
        compiler_params=pltpu.CompilerParams(
            dimension_semantics=("arbitrary", "arbitrary"), vmem_limit_bytes=VMEM_LIMIT),
        name="proj",
    )(x, g, wrow, wcol, wf, bf_col, pq, pk, oq, ok)


def _t5_buckets(n):
    max_exact = N_BUCKETS // 2
    dist = np.arange(n)
    d = np.maximum(dist, 1).astype(np.float32)
    large = np.float32(max_exact) + (np.log(d / np.float32(max_exact))
                                     / np.float32(math.log(MAX_DISTANCE / max_exact))
                                     * np.float32(N_BUCKETS - max_exact))
    large = np.minimum(large.astype(np.int32), N_BUCKETS - 1)
    return np.where(dist < max_exact, dist, large).astype(np.int32)


def _bias_kernel(rel_ref, idx_ref, out_ref):
    hd = pl.program_id(0)
    idx = idx_ref[...]
    val = jnp.full(idx.shape, NEG, jnp.float32)
    for bkt in range(N_BUCKETS):
        val = jnp.where(idx == bkt, rel_ref[bkt, hd] * LOG2E, val)
    out_ref[0] = val


def _bias_tiles(rel_bias, S):
    buckets = _t5_buckets(S)
    far = int(buckets[-1])
    assert np.all(buckets[TQ + 1:] == far)
    s_loc = np.arange(TK)[:, None]
    t_loc = np.arange(TQ)[None, :]
    rel0 = t_loc - s_loc
    idx0 = np.where(rel0 >= 0, buckets[np.maximum(rel0, 0)], -1)
    idx1 = buckets[TQ + t_loc - s_loc]
    idx = jnp.asarray(np.concatenate([idx1, idx0], axis=0).astype(np.int32))
    tiles = pl.pallas_call(
        _bias_kernel,
        grid=(DIFF_HEADS,),
        in_specs=[pl.BlockSpec(memory_space=pltpu.SMEM),
                  pl.BlockSpec((2 * TK, TQ), lambda h: (0, 0))],
        out_specs=pl.BlockSpec((1, 2 * TK, TQ), lambda h: (h, 0, 0)),
        out_shape=jax.ShapeDtypeStruct((DIFF_HEADS, 2 * TK, TQ), jnp.float32),
        name="t5_bias_tiles",
    )(rel_bias, idx)
    return tiles, far


def _flash_update(m_ref, acc_ref, a, st, vt, shift=None):
    m_old = m_ref[a]
    mx = jnp.max(st, axis=0, keepdims=True)
    if shift is None:
        m_new = jnp.maximum(m_old, mx)
        x = st - m_new
    else:
        m_new = jnp.maximum(m_old, mx + shift)
        x = st - (m_new - shift)
    p = jnp.exp2(x.astype(jnp.bfloat16))
    alpha = jnp.exp2(m_old - m_new)
    acc_ref[a] = alpha * acc_ref[a] + jnp.dot(vt, p, preferred_element_type=jnp.float32)
    m_ref[a] = m_new


def _flash_init(m_ref, acc_ref):
    m_ref[...] = jnp.full(m_ref.shape, NEG, jnp.float32)
    acc_ref[...] = jnp.zeros_like(acc_ref)


def _run_chains(n, st_ref, score_fn, update_fn, j, rows, prefetch_fn, done_fn=None):
    def finish(item):
        update_fn(*item)
        if done_fn is not None:
            done_fn(item[0])

    def staged(c):
        if rows <= 2 * TK:
            return st_ref[c, :rows]
        return jnp.concatenate([st_ref[c], score_fn(c, j + 2, rows - 2 * TK)], axis=0)

    pending = [(c, staged(c)) for c in range(LOOKAHEAD)]
    for c in range(LOOKAHEAD, n):
        pending.append((c, score_fn(c, j, rows)))
        finish(pending.pop(0))
    st_ref[0] = prefetch_fn(0)
    for c in range(LOOKAHEAD):
        if c + 1 < LOOKAHEAD:
            st_ref[c + 1] = prefetch_fn(c + 1)
        finish(pending.pop(0))


def _value_rows(v_ref, j, rows, r0, nrows):
    blocks = [v_ref[0, j + b, r0:r0 + nrows] for b in range(rows // TK)]
    return blocks[0] if len(blocks) == 1 else jnp.concatenate(blocks, axis=1)


def _stage_keys(k_ref, kbuf_ref, qpad_ref, first):
    @pl.when(first)
    def _():
        def body(t, carry):
            r = pl.multiple_of(t * TK, TK)
            kbuf_ref[pl.ds(r, TK), :] = k_ref[0, pl.ds(r, TK), :]
            return carry

        lax.fori_loop(0, k_ref.shape[1] // TK, body, 0)
        qpad_ref[...] = jnp.zeros_like(qpad_ref)


def _first_step_prime(first, st_ref, fill_queries, q_ref, score_fn):
    @pl.when(first)
    def _():
        fill_queries(0, q_ref)
        for c in range(LOOKAHEAD):
            st_ref[c] = score_fn(c, 0, 2 * TK, 0)


def _key_block_schedule(i, step):
    nfar = jnp.maximum(i - 1, 0)
    odd = (nfar & 1) == 1

    def far_pair(jj, carry):
        step(2 * jj, 2 * TK, "far", 2 * jj + 2)
        return carry

    lax.fori_loop(0, lax.shift_right_logical(nfar, 1), far_pair, 0)

    @pl.when(odd)
    def _():
        step(i - 2, 3 * TK, "last", None)

    @pl.when(jnp.logical_and(i >= 1, jnp.logical_not(odd)))
    def _():
        step(i - 1, 2 * TK, "last", None)

    @pl.when(i == 0)
    def _():
        step(0, TK, "last", None)


def _diff_kernel(far_bucket, lambda_init,
                 rel_ref, k_ref, q_ref, qnext_ref, v_ref, bias_ref, lq1_ref, lk1_ref, lq2_ref, lk2_ref,
                 g_ref, o_ref, qpad_ref, m_ref, acc_ref, st_ref, kbuf_ref):
    i = pl.program_id(1)
    d = DIFF_HEAD_DIM
    cur, nxt = i & 1, (i + 1) & 1
    _stage_keys(k_ref, kbuf_ref, qpad_ref, i == 0)

    def fill_queries(slot, src_ref):
        for c in range(2 * DIFF_HEADS):
            r0 = d * (c % 4)
            qpad_ref[slot, c, r0:r0 + d] = src_ref[0, 0, d * c:d * (c + 1)]

    def score(c, j, rows, slot=cur):
        pair = c // 4
        kblk = kbuf_ref[pl.ds(pl.multiple_of(j * TK, TK), rows), 256 * pair:256 * (pair + 1)]
        return jnp.dot(kblk, qpad_ref[slot, c], preferred_element_type=jnp.float32)

    _first_step_prime(i == 0, st_ref, fill_queries, q_ref, score)
    fill_queries(nxt, qnext_ref)
    _flash_init(m_ref, acc_ref)

    def step(j, rows, kind, prefetch_j):
        if prefetch_j is None:
            prefetch = lambda c: score(c, 0, 2 * TK, nxt)
        else:
            prefetch = lambda c: score(c, prefetch_j, 2 * TK)

        def update(c, st):
            hd = c // 2
            vt = _value_rows(v_ref, j, rows, DV_ROWS * hd, DV_ROWS)
            far_bias = rel_ref[far_bucket, hd] * LOG2E
            if kind == "far":
                _flash_update(m_ref, acc_ref, c, st, vt, shift=far_bias)
            else:
                bias = bias_ref[hd, 2 * TK - min(rows, 2 * TK):]
                if rows > 2 * TK:
                    bias = jnp.concatenate(
                        [jnp.full((rows - 2 * TK, TQ), far_bias, jnp.float32), bias], axis=0)
                _flash_update(m_ref, acc_ref, c, bias + st, vt)

        done = (lambda c: emit_head(c // 2) if c % 2 == 1 else None) if prefetch_j is None else None
        _run_chains(2 * DIFF_HEADS, st_ref, score, update, j, rows, prefetch, done)

    def emit_head(hd):
        lam = (jnp.exp(jnp.sum(lq1_ref[...] * lk1_ref[...], axis=-1, keepdims=True))
               - jnp.exp(jnp.sum(lq2_ref[...] * lk2_ref[...], axis=-1, keepdims=True))
               + lambda_init)
        dv = 2 * d
        o0 = acc_ref[2 * hd, :dv] * (1.0 / acc_ref[2 * hd, dv:dv + 1])
        o1 = acc_ref[2 * hd + 1, :dv] * (1.0 / acc_ref[2 * hd + 1, dv:dv + 1])
        od = o0 - lam * o1
        ms = jnp.mean(od * od, axis=0, keepdims=True)
        y = (od * lax.rsqrt(ms + EPS)).T * g_ref[...]
        o_ref[0, :, 128 * hd:128 * (hd + 1)] = (y * (1.0 - lambda_init)).astype(o_ref.dtype)

    _key_block_schedule(i, step)


def _diff_attention(kd, qd, vd, bias_tiles, far_bucket, rel_bias, lq1, lk1, lq2, lk2, g, lambda_init):
    B, S, _ = kd.shape
    nq = S // TQ
    nmaps = 2 * DIFF_HEADS
    vec = lambda n: pl.BlockSpec((1, n), lambda b, i: (0, 0))
    return pl.pallas_call(
        functools.partial(_diff_kernel, far_bucket, lambda_init),
        grid=(B, nq),
        in_specs=[
            pl.BlockSpec(memory_space=pltpu.SMEM),
            pl.BlockSpec((1, S, DIFF_WIDTH), lambda b, i: (b, 0, 0)),
            pl.BlockSpec((1, 1, DIFF_WIDTH, TQ), lambda b, i: (b, i, 0, 0)),
            pl.BlockSpec((1, 1, DIFF_WIDTH, TQ), lambda b, i: (b, jnp.minimum(i + 1, nq - 1), 0, 0)),
            pl.BlockSpec((1, nq, DIFF_HEADS * DV_ROWS, TK), lambda b, i: (b, 0, 0, 0)),
            pl.BlockSpec((DIFF_HEADS, 2 * TK, TQ), lambda b, i: (0, 0, 0)),
            vec(DIFF_HEAD_DIM), vec(DIFF_HEAD_DIM), vec(DIFF_HEAD_DIM), vec(DIFF_HEAD_DIM),
            vec(2 * DIFF_HEAD_DIM),
        ],
        out_specs=pl.BlockSpec((1, TQ, DIFF_WIDTH), lambda b, i: (b, i, 0)),
        out_shape=jax.ShapeDtypeStruct((B, S, DIFF_WIDTH), jnp.bfloat16),
        scratch_shapes=[
            pltpu.VMEM((2, nmaps, 256, TQ), jnp.bfloat16),
            pltpu.VMEM((nmaps, 1, TQ), jnp.float32),
            pltpu.VMEM((nmaps, DV_ROWS, TQ), jnp.float32),
            pltpu.VMEM((LOOKAHEAD, 2 * TK, TQ), jnp.float32),
            pltpu.VMEM((S, DIFF_WIDTH), jnp.bfloat16),
        ],
        compiler_params=pltpu.CompilerParams(
            dimension_semantics=("arbitrary", "arbitrary"), vmem_limit_bytes=VMEM_LIMIT),
        name="diff_attention",
    )(rel_bias, kd, qd, qd, vd, bias_tiles, lq1, lk1, lq2, lk2, g)


def _fox_kernel(k_ref, q_ref, xq_ref, qnext_ref, xqnext_ref, v_ref, o_ref,
                qpad_ref, m_ref, acc_ref, st_ref, kbuf_ref):
    i = pl.program_id(1)
    d = FOX_HEAD_DIM
    cur, nxt = i & 1, (i + 1) & 1
    _stage_keys(k_ref, kbuf_ref, qpad_ref, i == 0)

    def fill_queries(slot, src_refs):
        qsrc_ref, xsrc_ref = src_refs
        for hd in range(FOX_HEADS):
            a = hd % 2
            qpad_ref[slot, hd, d * a:d * (a + 1)] = qsrc_ref[0, 0, d * hd:d * (hd + 1)]
            qpad_ref[slot, hd, 128 + XROWS * a:128 + XROWS * (a + 1)] = (
                xsrc_ref[0, 0, XROWS * hd:XROWS * (hd + 1)])

    def score(hd, j, rows, slot=cur):
        u = hd // 2
        kblk = kbuf_ref[pl.ds(pl.multiple_of(j * TK, TK), rows), 256 * u:256 * (u + 1)]
        return jnp.dot(kblk, qpad_ref[slot, hd], preferred_element_type=jnp.float32)

    _first_step_prime(i == 0, st_ref, fill_queries, (q_ref, xq_ref), score)
    fill_queries(nxt, (qnext_ref, xqnext_ref))
    _flash_init(m_ref, acc_ref)

    def step(j, rows, kind, prefetch_j):
        if prefetch_j is None:
            prefetch = lambda hd: score(hd, 0, 2 * TK, nxt)
        else:
            prefetch = lambda hd: score(hd, prefetch_j, 2 * TK)

        def update(hd, st):
            if kind != "far":
                row = lax.broadcasted_iota(jnp.int32, st.shape, 0) - (rows - TK)
                col = lax.broadcasted_iota(jnp.int32, st.shape, 1)
                st = jnp.where(row > col, NEG, st)
            _flash_update(m_ref, acc_ref, hd, st, _value_rows(v_ref, j, rows, FV_ROWS * hd, FV_ROWS))

        done = (lambda hd: emit_pair(hd // 2) if hd % 2 == 1 else None) if prefetch_j is None else None
        _run_chains(FOX_HEADS, st_ref, score, update, j, rows, prefetch, done)

    def emit_pair(u):
        o = jnp.concatenate([acc_ref[hd, :d] * (1.0 / acc_ref[hd, d:d + 1]) for hd in (2 * u, 2 * u + 1)],
                            axis=0)
        o_ref[0, :, 128 * u:128 * (u + 1)] = o.T.astype(o_ref.dtype)

    _key_block_schedule(i, step)


def _fox_attention(kf, qf, xq, vf):
    B, S, _ = kf.shape
    nq = S // TQ
    return pl.pallas_call(
        _fox_kernel,
        grid=(B, nq),
        in_specs=[
            pl.BlockSpec((1, S, 2 * FOX_WIDTH), lambda b, i: (b, 0, 0)),
            pl.BlockSpec((1, 1, FOX_WIDTH, TQ), lambda b, i: (b, i, 0, 0)),
            pl.BlockSpec((1, 1, FOX_HEADS * XROWS, TQ), lambda b, i: (b, i, 0, 0)),
            pl.BlockSpec((1, 1, FOX_WIDTH, TQ), lambda b, i: (b, jnp.minimum(i + 1, nq - 1), 0, 0)),
            pl.BlockSpec((1, 1, FOX_HEADS * XROWS, TQ), lambda b, i: (b, jnp.minimum(i + 1, nq - 1), 0, 0)),
            pl.BlockSpec((1, nq, FOX_HEADS * FV_ROWS, TK), lambda b, i: (b, 0, 0, 0)),
        ],
        out_specs=pl.BlockSpec((1, TQ, FOX_WIDTH), lambda b, i: (b, i, 0)),
        out_shape=jax.ShapeDtypeStruct((B, S, FOX_WIDTH), jnp.bfloat16),
        scratch_shapes=[
            pltpu.VMEM((2, FOX_HEADS, 256, TQ), jnp.bfloat16),
            pltpu.VMEM((FOX_HEADS, 1, TQ), jnp.float32),
            pltpu.VMEM((FOX_HEADS, FV_ROWS, TQ), jnp.float32),
            pltpu.VMEM((LOOKAHEAD, 2 * TK, TQ), jnp.float32),
            pltpu.VMEM((S, 2 * FOX_WIDTH), jnp.bfloat16),
        ],
        compiler_params=pltpu.CompilerParams(
            dimension_semantics=("arbitrary", "arbitrary"), vmem_limit_bytes=VMEM_LIMIT),
        name="fox_attention",
    )(kf, qf, xq, qf, xq, vf)


def _post_kernel(md_ref, mf_ref, x_ref, wod_ref, wof_ref, wup_ref, wdn_ref, g1_ref, g2_ref, g3_ref,
                 o_ref):
    subs = [slice(s * POST_SUB, (s + 1) * POST_SUB) for s in range(x_ref.shape[0] // POST_SUB)]
    attn = [jnp.dot(md_ref[rows], wod_ref[...], preferred_element_type=jnp.float32)
            + jnp.dot(mf_ref[rows], wof_ref[...], preferred_element_type=jnp.float32) for rows in subs]
    for rows, a in zip(subs, attn):
        x1 = x_ref[rows] + _rms(a, g1_ref[...])
        h2 = _rms(x1, g2_ref[...]).astype(jnp.bfloat16)
        acc = None
        for c in range(D_FF // FF_CHUNK):
            u = jnp.dot(h2, wup_ref[:, c * FF_CHUNK:(c + 1) * FF_CHUNK], preferred_element_type=jnp.float32)
            u = jnp.square(jnp.maximum(u, 0.0)).astype(jnp.bfloat16)
            part = jnp.dot(u, wdn_ref[c * FF_CHUNK:(c + 1) * FF_CHUNK, :], preferred_element_type=jnp.float32)
            acc = part if acc is None else acc + part
        o_ref[rows] = x1 + _rms(acc, g3_ref[...])


def _post(mix_d, mix_f, x2d, wod, wof, wup, wdn, g1, g2, g3):
    R, D = x2d.shape
    tm = TM_POST
    resident = lambda shape: pl.BlockSpec(shape, lambda r: (0,) * len(shape),
                                          pipeline_mode=pl.Buffered(1))
    return pl.pallas_call(
        _post_kernel,
        grid=(R // tm,),
        in_specs=[
            pl.BlockSpec((tm, DIFF_WIDTH), lambda r: (r, 0)),
            pl.BlockSpec((tm, FOX_WIDTH), lambda r: (r, 0)),
            pl.BlockSpec((tm, D), lambda r: (r, 0)),
            resident(wod.shape), resident(wof.shape), resident(wup.shape), resident(wdn.shape),
            resident((1, D)), resident((1, D)), resident((1, D)),
        ],
        out_specs=pl.BlockSpec((tm, D), lambda r: (r, 0)),
        out_shape=jax.ShapeDtypeStruct((R, D), jnp.float32),
        compiler_params=pltpu.CompilerParams(
            dimension_semantics=("parallel",), vmem_limit_bytes=VMEM_LIMIT,
            allow_input_fusion=[False, False, False, True, True, True, True, False, False, False]),
        name="post",
    )(mix_d, mix_f, x2d, wod, wof, wup, wdn, g1, g2, g3)


def kernel(x, ln_attn_pre, w_in, b_f, lam_q1, lam_k1, lam_q2, lam_k2, subln_g, rel_bias,
           w_out, ln_attn_post, ln_mlp_pre, w_up, w_down, ln_mlp_post):
    B, S, D = x.shape
    depth = w_in.shape[0]
    bf16 = jnp.bfloat16
    bias_tiles, far_bucket = _bias_tiles(rel_bias, S)
    dw, fw = DIFF_WIDTH, FOX_WIDTH
    for l in range(depth):
        lambda_init = 0.8 - 0.6 * math.exp(-0.3 * l)
        w = w_in[l]
        o_dq, o_dk, o_dv, o_fq, o_fk, o_fv, o_g = 0, dw, 2 * dw, 3 * dw, 3 * dw + fw, 3 * dw + 2 * fw, 3 * dw + 3 * fw
        wrow = jnp.concatenate([w[:, o_dk:o_dk + dw], w[:, o_fk:o_fk + fw]], axis=1).astype(bf16)
        wcol = jnp.concatenate([w[:, o_dq:o_dq + dw], w[:, o_dv:o_dv + dw],
                                w[:, o_fq:o_fq + fw], w[:, o_fv:o_fv + fw]], axis=1).T.astype(bf16)
        wf = jnp.zeros((XROWS, D), bf16).at[:FOX_HEADS].set(w[:, o_g:o_g + FOX_HEADS].T.astype(bf16))
        bf_col = jnp.zeros((XROWS, 1), jnp.float32).at[:FOX_HEADS, 0].set(b_f[l].astype(jnp.float32))

        kd, kf, qd, vd, qf, vf, xq = _proj(x, ln_attn_pre[l][None, :], wrow, wcol, wf, bf_col)
        mix_d = _diff_attention(kd, qd, vd, bias_tiles, far_bucket, rel_bias,
                                lam_q1[l][None, :], lam_k1[l][None, :], lam_q2[l][None, :],
                                lam_k2[l][None, :], subln_g[l][None, :], lambda_init)
        mix_f = _fox_attention(kf, qf, xq, vf)
        wo = w_out[l].astype(bf16)
        y = _post(mix_d.reshape(B * S, dw), mix_f.reshape(B * S, fw), x.reshape(B * S, D),
                  wo[:dw], wo[dw:], w_up[l].astype(bf16), w_down[l].astype(bf16),
                  ln_attn_post[l][None, :], ln_mlp_pre[l][None, :], ln_mlp_post[l][None, :])
        x = y.reshape(B, S, D)
    return x
```
